```python
import math
import jax, jax.numpy as jnp
from jax import lax
import numpy as np

D_MODEL = 1024
BATCH = 16
SEQ = 2048
DEPTH = 4

HEAD_DIM = 64
NSA_Q_HEADS = 6
NSA_KV_HEADS = 2
NSA_HPG = NSA_Q_HEADS // NSA_KV_HEADS
NSA_WIDTH = NSA_Q_HEADS * HEAD_DIM
N_BRANCH = 3
NSA_KV_COLS = N_BRANCH * 2 * NSA_KV_HEADS * HEAD_DIM
CMP_BLOCK = 32
CMP_STRIDE = 16
CMP_HIDDEN = 64
SEL_BLOCK = 64
SEL_TOP_N = 16
WINDOW = 512
FORCE_SCORE = 1.0e4
GLA_HEADS = 6
GLA_DK = 32
GLA_DV = 64
GLA_WIDTH = GLA_HEADS * GLA_DV
GLA_GATE_RANK = 16
GLA_GATE_NORM = 16.0
GLA_CHUNK = 64
POOL_WINDOWS = (2, 4, 8, 16)
POOL_GROUPS = 4
POOL_GROUP = 64
POOL_WIDTH = POOL_GROUPS * POOL_GROUP
MIX_WIDTH = NSA_WIDTH + GLA_WIDTH + POOL_WIDTH
SPLIT_SIZES = (NSA_WIDTH, NSA_KV_COLS, NSA_Q_HEADS * N_BRANCH,
               GLA_HEADS * GLA_DK, GLA_HEADS * GLA_DK, GLA_WIDTH, GLA_GATE_RANK, GLA_WIDTH,
               POOL_WIDTH)
P_IN = sum(SPLIT_SIZES)
D_FF = 4 * D_MODEL
ROPE_THETA = 500000.0
ROPE_DIM = HEAD_DIM // 4
Q_BLOCK = 128
SEL_Q_BLOCK = 32
LN_EPS = 1e-5
RMS_EPS = 1e-6
DEEPNORM_ALPHA = (2 * DEPTH) ** 0.25
DEEPNORM_BETA = (8 * DEPTH) ** -0.25

kernel_name = "hymba_nsa_gla_pool_deepnorm"


def partial_rope(x, pos):
    half = ROPE_DIM // 2
    inv = ROPE_THETA ** (-jnp.arange(half, dtype=jnp.float32) * 2.0 / ROPE_DIM)
    ang = pos.astype(jnp.float32)[:, None] * inv[None, :]
    cos = jnp.cos(ang).astype(x.dtype)
    sin = jnp.sin(ang).astype(x.dtype)
    x1, x2, xp = x[..., :half], x[..., half:ROPE_DIM], x[..., ROPE_DIM:]
    return jnp.concatenate([x1 * cos - x2 * sin, x1 * sin + x2 * cos, xp], axis=-1)


def layer_norm(x, g, b):
    xf = x.astype(jnp.float32)
    mu = jnp.mean(xf, -1, keepdims=True)
    var = jnp.mean(jnp.square(xf - mu), -1, keepdims=True)
    return ((xf - mu) * lax.rsqrt(var + LN_EPS)).astype(x.dtype) * g + b


def masked_softmax(s, mask):
    s = jnp.where(mask, s.astype(jnp.float32), -jnp.inf)
    m = jnp.max(s, -1, keepdims=True)
    m = jnp.where(jnp.isfinite(m), m, 0.0)
    e = jnp.exp(s - m)
    return e / jnp.maximum(jnp.sum(e, -1, keepdims=True), 1e-30)


def nsa_mixer(q, kv, gates, cmp_pos, cmp_w1, cmp_w2):
    B, S, _ = q.shape
    G, H, Dh = NSA_KV_HEADS, NSA_HPG, HEAD_DIM
    pos = jnp.arange(S)
    q = q.reshape(B, S, G, H, Dh).transpose(0, 2, 3, 1, 4)
    q = partial_rope(q, pos) * (Dh ** -0.5)
    kv = kv.reshape(B, S, N_BRANCH, 2, G, Dh).transpose(2, 3, 0, 4, 1, 5)

    n_cmp = (S - CMP_BLOCK) // CMP_STRIDE + 1
    blk_idx = CMP_STRIDE * jnp.arange(n_cmp)[:, None] + jnp.arange(CMP_BLOCK)[None, :]

    def compress(t, p, w1, w2):
        tb = t[:, :, blk_idx] + p
        hid = jax.nn.gelu(tb.reshape(B, G, n_cmp, CMP_BLOCK * Dh) @ w1)
        return hid @ w2

    kc = compress(partial_rope(kv[0, 0], pos), cmp_pos[0], cmp_w1[0], cmp_w2[0])
    vc = compress(kv[0, 1], cmp_pos[1], cmp_w1[1], cmp_w2[1])
    s_cmp = jnp.einsum('bghsd,bgnd->bghsn', q, kc)
    cmp_mask = blk_idx[:, -1][None, :] <= pos[:, None]
    p_cmp = masked_softmax(s_cmp, cmp_mask)
    o_cmp = jnp.einsum('bghsn,bgnd->bghsd', p_cmp.astype(vc.dtype), vc)

    n_sel = S // SEL_BLOCK
    n_top = min(SEL_TOP_N, n_sel)
    c_lo = blk_idx[:, 0:1]
    s_lo = (SEL_BLOCK * jnp.arange(n_sel))[None, :]
    overlap = jnp.clip(jnp.minimum(c_lo + CMP_BLOCK, s_lo + SEL_BLOCK) - jnp.maximum(c_lo, s_lo), 0, None)
    cmp_to_sel = overlap.astype(jnp.float32) / CMP_BLOCK
    imp = jnp.einsum('bghsn,nj->bgsj', p_cmp, cmp_to_sel)
    cur = (pos // SEL_BLOCK)[:, None]
    jb = jnp.arange(n_sel)[None, :]
    imp = jnp.where((jb == 0) | (jb == cur) | (jb == cur - 1), FORCE_SCORE, imp)
    imp = jnp.where(jb > cur, -FORCE_SCORE, imp)
    _, sel_idx = lax.top_k(imp, n_top)

    k_slc = partial_rope(kv[1, 0], pos).reshape(B, G, n_sel, SEL_BLOCK, Dh)
    v_slc = kv[1, 1].reshape(B, G, n_sel, SEL_BLOCK, Dh)
    nqb = S // SEL_Q_BLOCK
    q_blk = q.reshape(B, G, H, nqb, SEL_Q_BLOCK, Dh).transpose(3, 0, 1, 2, 4, 5)
    idx_blk = sel_idx.reshape(B, G, nqb, SEL_Q_BLOCK, n_top).transpose(2, 0, 1, 3, 4)
    t_blk = pos.reshape(nqb, SEL_Q_BLOCK)
    bi = jnp.arange(B)[:, None, None, None]
    gi = jnp.arange(G)[None, :, None, None]

    def sel_block(args):
        qb, ib, tb = args
        kb = k_slc[bi, gi, ib]
        vb = v_slc[bi, gi, ib]
        s = jnp.einsum('bghqd,bgqnkd->bghqnk', qb, kb)
        kpos = ib[..., None] * SEL_BLOCK + jnp.arange(SEL_BLOCK)
        m = (kpos <= tb[None, None, :, None, None])[:, :, None]
        s = s.reshape(B, G, H, SEL_Q_BLOCK, n_top * SEL_BLOCK)
        m = m.reshape(B, G, 1, SEL_Q_BLOCK, n_top * SEL_BLOCK)
        p = masked_softmax(s, m).reshape(B, G, H, SEL_Q_BLOCK, n_top, SEL_BLOCK)
        return jnp.einsum('bghqnk,bgqnkd->bghqd', p.astype(vb.dtype), vb)

    o_slc = lax.map(sel_block, (q_blk, idx_blk, t_blk))
    o_slc = o_slc.transpose(1, 2, 3, 0, 4, 5).reshape(B, G, H, S, Dh)

    pad = ((0, 0), (0, 0), (WINDOW, 0), (0, 0))
    k_win = jnp.pad(partial_rope(kv[2, 0], pos), pad)
    v_win = jnp.pad(kv[2, 1], pad)
    nwb = S // Q_BLOCK
    span = WINDOW + Q_BLOCK
    q_wb = q.reshape(B, G, H, nwb, Q_BLOCK, Dh).transpose(3, 0, 1, 2, 4, 5)
    starts = jnp.arange(nwb) * Q_BLOCK

    def win_block(args):
        qb, s0 = args
        kb = lax.dynamic_slice_in_dim(k_win, s0, span, axis=2)
        vb = lax.dynamic_slice_in_dim(v_win, s0, span, axis=2)
        s = jnp.einsum('bghqd,bgkd->bghqk', qb, kb)
        tq = s0 + jnp.arange(Q_BLOCK)
        tk = s0 - WINDOW + jnp.arange(span)
        diff = tq[:, None] - tk[None, :]
        m = (diff >= 0) & (diff < WINDOW) & (tk[None, :] >= 0)
        p = masked_softmax(s, m)
        return jnp.einsum('bghqk,bgkd->bghqd', p.astype(vb.dtype), vb)

    o_win = lax.map(win_block, (q_wb, starts))
    o_win = o_win.transpose(1, 2, 3, 0, 4, 5).reshape(B, G, H, S, Dh)

    gt = jax.nn.sigmoid(gates.reshape(B, S, G, H, N_BRANCH).transpose(0, 2, 3, 1, 4))
    o = gt[..., 0:1] * o_cmp + gt[..., 1:2] * o_slc + gt[..., 2:3] * o_win
    return o.transpose(0, 3, 1, 2, 4).reshape(B, S, NSA_WIDTH)


def gla_mixer(q, k, v, g_lr, og, w_gate2, b_gate, norm_g):
    B, S, _ = q.shape
    Hh, dk, dv, C = GLA_HEADS, GLA_DK, GLA_DV, GLA_CHUNK
    N = S // C
    f32 = jnp.float32
    gk = jax.nn.log_sigmoid((g_lr @ w_gate2 + b_gate).astype(f32)) / GLA_GATE_NORM

    def chunks(t, d):
        return t.astype(f32).reshape(B, N, C, Hh, d).transpose(0, 3, 1, 2, 4)

    qc = chunks(q, dk) * (dk ** -0.5)
    kc = chunks(k, dk)
    vc = chunks(v, dv)
    bcum = jnp.cumsum(chunks(gk, dk), axis=3)
    blast = bcum[:, :, :, -1:]
    q_dec = qc * jnp.exp(bcum)
    k_inv = kc * jnp.exp(-bcum)
    k_end = kc * jnp.exp(blast - bcum)
    causal = jnp.tril(jnp.ones((C, C), bool))
    a = jnp.where(causal, jnp.einsum('bhnid,bhnjd->bhnij', q_dec, k_inv), 0.0)
    o_intra = jnp.einsum('bhnij,bhnjv->bhniv', a, vc)
    d_state = jnp.einsum('bhncd,bhncv->bhndv', k_end, vc)
    decay = jnp.exp(blast[:, :, :, 0])

    def step(s_prev, inp):
        dcy, ds = inp
        return s_prev * dcy[..., None] + ds, s_prev

    _, s_in = lax.scan(step, jnp.zeros((B, Hh, dk, dv), f32),
                       (decay.transpose(2, 0, 1, 3), d_state.transpose(2, 0, 1, 3, 4)))
    s_in = s_in.transpose(1, 2, 0, 3, 4)
    o = o_intra + jnp.einsum('bhncd,bhndv->bhncv', q_dec, s_in)
    o = o * lax.rsqrt(jnp.mean(o * o, -1, keepdims=True) + RMS_EPS)
    o = o.transpose(0, 2, 3, 1, 4).reshape(B, S, GLA_WIDTH).astype(og.dtype)
    return o * norm_g * jax.nn.silu(og)


def pool_mixer(u, w_pool, scale):
    B, S, Cw = u.shape
    uf = u.astype(jnp.float32)
    cs = jnp.pad(jnp.cumsum(uf, axis=1), ((0, 0), (1, 0), (0, 0)))
    win = jnp.repeat(jnp.array(POOL_WINDOWS, jnp.int32), POOL_GROUP)
    t = jnp.arange(S)[:, None]
    lo = jnp.maximum(t + 1 - win[None, :], 0)
    cnt = jnp.minimum(t + 1, win[None, :]).astype(jnp.float32)
    lo_sum = jnp.take_along_axis(cs, jnp.broadcast_to(lo[None], (B, S, Cw)), axis=1)
    pooled = (cs[:, 1:] - lo_sum) / cnt - uf
    y = jnp.einsum('bsgc,gcd->bsgd', pooled.reshape(B, S, POOL_GROUPS, POOL_GROUP).astype(u.dtype), w_pool)
    return y.reshape(B, S, POOL_WIDTH) * scale


def setup_inputs(seed: int = 0) -> dict:
    key = jax.random.key(seed)
    ks = jax.random.split(key, 20)
    L, D = DEPTH, D_MODEL
    nrm = lambda k, shape, s: jax.random.normal(k, shape, jnp.float32) * s
    return {
        "x": nrm(ks[0], (BATCH, SEQ, D), 1.0),
        "w_in": nrm(ks[1], (L, D, P_IN), D ** -0.5),
        "cmp_pos": nrm(ks[2], (L, 2, CMP_BLOCK, HEAD_DIM), 0.1),
        "cmp_w1": nrm(ks[3], (L, 2, CMP_BLOCK * HEAD_DIM, CMP_HIDDEN), (CMP_BLOCK * HEAD_DIM) ** -0.5),
        "cmp_w2": nrm(ks[4], (L, 2, CMP_HIDDEN, HEAD_DIM), CMP_HIDDEN ** -0.5),
        "gla_w_gate2": nrm(ks[5], (L, GLA_GATE_RANK, GLA_HEADS * GLA_DK), GLA_GATE_RANK ** -0.5),
        "gla_b_gate": nrm(ks[6], (L, GLA_HEADS * GLA_DK), 0.01),
        "gla_norm_g": 1.0 + nrm(ks[7], (L, GLA_WIDTH), 0.02),
        "pool_w": nrm(ks[8], (L, POOL_GROUPS, POOL_GROUP, POOL_GROUP), POOL_GROUP ** -0.5),
        "pool_scale": 1.0 + nrm(ks[9], (L, POOL_WIDTH), 0.02),
        "w_out": nrm(ks[10], (L, MIX_WIDTH, D), MIX_WIDTH ** -0.5 * DEEPNORM_BETA),
        "ln1_g": 1.0 + nrm(ks[11], (L, D), 0.02),
        "ln1_b": nrm(ks[12], (L, D), 0.02),
        "w_up": nrm(ks[13], (L, D, D_FF), D ** -0.5),
        "w_down": nrm(ks[14], (L, D_FF, D), D_FF ** -0.5 * DEEPNORM_BETA),
        "ln2_g": 1.0 + nrm(ks[15], (L, D), 0.02),
        "ln2_b": nrm(ks[16], (L, D), 0.02),
    }


def reference(x, w_in, cmp_pos, cmp_w1, cmp_w2, gla_w_gate2, gla_b_gate, gla_norm_g, pool_w, pool_scale,
              w_out, ln1_g, ln1_b, w_up, w_down, ln2_g, ln2_b):
    split_points = []
    acc = 0
    for sz in SPLIT_SIZES[:-1]:
        acc += sz
        split_points.append(acc)
    h = x
    for l in range(DEPTH):
        proj = h @ w_in[l]
        nq, nkv, ngate, gq, gk, gv, glr, gog, pu = jnp.split(proj, split_points, axis=-1)
        mixed = jnp.concatenate([
            nsa_mixer(nq, nkv, ngate, cmp_pos[l], cmp_w1[l], cmp_w2[l]),
            gla_mixer(gq, gk, gv, glr, gog, gla_w_gate2[l], gla_b_gate[l], gla_norm_g[l]),
            pool_mixer(pu, pool_w[l], pool_scale[l]),
        ], axis=-1)
        h = layer_norm(DEEPNORM_ALPHA * h + mixed @ w_out[l], ln1_g[l], ln1_b[l])
        ff = jnp.square(jax.nn.relu(h @ w_up[l])) @ w_down[l]
        h = layer_norm(DEEPNORM_ALPHA * h + ff, ln2_g[l], ln2_b[l])
    return h
```

```python
import functools
import numpy as np
import jax
import jax.numpy as jnp
from jax import lax
from jax.experimental import pallas as pl
from jax.experimental.pallas import tpu as pltpu

F32 = jnp.float32
BF16 = jnp.bfloat16

D_MODEL = 1024
HEAD_DIM = 64
NSA_G = 2
NSA_H = 3
NSA_WIDTH = NSA_G * NSA_H * HEAD_DIM
CMP_BLOCK = 32
CMP_STRIDE = 16
SEL_BLOCK = 64
SEL_TOP_N = 16
WINDOW = 512
FORCE_SCORE = 1.0e4
GLA_HEADS = 6
GLA_DK = 32
GLA_DV = 64
GLA_WIDTH = GLA_HEADS * GLA_DV
GLA_RANK = 16
GLA_GATE_NORM = 16.0
GLA_CHUNK = 64
GLA_QK_PAD = 256
POOL_WINDOWS = (2, 4, 8, 16)
POOL_GROUP = 64
POOL_WIDTH = 256
D_FF = 4 * D_MODEL
ROPE_THETA = 500000.0
ROPE_DIM = HEAD_DIM // 4
LN_EPS = 1e-5
RMS_EPS = 1e-6
DEPTH = 4
ALPHA = (2 * DEPTH) ** 0.25

LANE = 128
NEG = -1e30
SEL_NEG = -30000.0

VMEM_LIMIT = 48 * 1024 * 1024

SMALL_GATE0 = 0
SMALL_GLR0 = 32
PROJ_COLS = 22 * LANE


def _nt(a, b):
    return lax.dot_general(a, b, (((1,), (1,)), ((), ())), preferred_element_type=F32)


def _tn(a, b):
    return lax.dot_general(a, b, (((0,), (0,)), ((), ())), preferred_element_type=F32)


def _dot(a, b):
    return jnp.dot(a, b, preferred_element_type=F32)


def _split3(x):
    hi = x.astype(BF16)
    r = x - hi.astype(F32)
    mid = r.astype(BF16)
    lo = (r - mid.astype(F32)).astype(BF16)
    return hi, mid, lo


def _layer_norm(y, g, b):
    mu = jnp.mean(y, axis=-1, keepdims=True)
    d = y - mu
    var = jnp.mean(d * d, axis=-1, keepdims=True)
    return d * lax.rsqrt(var + LN_EPS) * g + b


def _rope(x, a, bm, cm):
    return x * a + pltpu.roll(x, LANE - ROPE_DIM // 2, 1) * bm + pltpu.roll(x, ROPE_DIM // 2, 1) * cm


def _proj_kernel(h_ref, w_ref, qa_ref, qb_ref, qc_ref, ka_ref, kb_ref, kc_ref,
                 q_out, cmpk_out, cmpv_out, sel_out, win_out, small_out, gq_out, gk_out, gv_out, og_out, pu_out):
    hb = h_ref[...].astype(BF16)
    plan = ([(q_out, j, "q") for j in range(3)]
            + [(cmpk_out, 0, "k"), (cmpv_out, 0, None), (sel_out, 0, "k"), (sel_out, 1, None),
               (win_out, 0, "k"), (win_out, 1, None), (small_out, 0, None)]
            + [(gq_out, j, None) for j in range(2)] + [(gk_out, j, None) for j in range(2)]
            + [(gv_out, j, None) for j in range(3)] + [(og_out, j, None) for j in range(3)]
            + [(pu_out, j, None) for j in range(2)])
    for c in range(0, len(plan), 2):
        acc = _dot(hb, w_ref[:, c * LANE:(c + 2) * LANE])
        for half in range(2):
            ref, j, kind = plan[c + half]
            x = acc[:, half * LANE:(half + 1) * LANE]
            if kind == "q":
                x = _rope(x, qa_ref[...], qb_ref[...], qc_ref[...])
            elif kind == "k":
                x = _rope(x, ka_ref[...], kb_ref[...], kc_ref[...])
            ref[:, j * LANE:(j + 1) * LANE] = x.astype(ref.dtype)


def _proj(h, w, rope_q, rope_k, seq, tm):
    m = h.shape[0]
    tiles_per_seq = seq // tm
    row = lambda i: (i, 0)
    tab = lambda i: (i % tiles_per_seq, 0)
    widths = [(NSA_WIDTH, BF16), (LANE, F32), (LANE, F32), (256, BF16), (256, BF16), (LANE, F32),
              (GLA_QK_PAD, BF16), (GLA_QK_PAD, BF16), (GLA_WIDTH, BF16), (GLA_WIDTH, BF16), (POOL_WIDTH, BF16)]
    return pl.pallas_call(
        _proj_kernel,
        grid=(m // tm,),
        in_specs=[pl.BlockSpec((tm, D_MODEL), row),
                  pl.BlockSpec((D_MODEL, PROJ_COLS), lambda i: (0, 0), pipeline_mode=pl.Buffered(1))]
                 + [pl.BlockSpec((tm, LANE), tab)] * 6,
        out_specs=[pl.BlockSpec((tm, wd), row) for wd, _ in widths],
        out_shape=[jax.ShapeDtypeStruct((m, wd), dt) for wd, dt in widths],
        compiler_params=pltpu.CompilerParams(dimension_semantics=("parallel",), vmem_limit_bytes=VMEM_LIMIT),
        name="in_proj",
    )(h, w, *rope_q, *rope_k)


def _cmp_kernel(k_ref, v_ref, pos_ref, w1_ref, w2_ref, o_ref):
    nblk = o_ref.shape[0]
    half = CMP_BLOCK // 2
    for kvi, x_ref in enumerate((k_ref, v_ref)):
        lanes = slice(kvi * LANE, (kvi + 1) * LANE)
        ha = jnp.zeros((nblk, LANE), F32)
        hb = jnp.zeros((nblk, LANE), F32)
        for r in range(half):
            t = x_ref[pl.ds(r, nblk, stride=CMP_STRIDE), :]
            ha = ha + _dot((t + pos_ref[kvi, r:r + 1, :]).astype(BF16), w1_ref[kvi, r])
            hb = hb + _dot((t + pos_ref[kvi, half + r:half + r + 1, :]).astype(BF16), w1_ref[kvi, half + r])
        hid = ha + pltpu.roll(hb, nblk - 1, 0)
        hid = jax.nn.gelu(hid, approximate=True)
        o_ref[:, lanes] = _dot(hid.astype(BF16), w2_ref[kvi])


def _compress(cmp_k, cmp_v, pos2, w1bd, w2bd, batch, seq):
    nblk = seq // CMP_STRIDE
    return pl.pallas_call(
        _cmp_kernel,
        grid=(batch,),
        in_specs=[pl.BlockSpec((seq, LANE), lambda b: (b, 0)), pl.BlockSpec((seq, LANE), lambda b: (b, 0)),
                  pl.BlockSpec((2, CMP_BLOCK, LANE), lambda b: (0, 0, 0)),
                  pl.BlockSpec((2, CMP_BLOCK, LANE, LANE), lambda b: (0, 0, 0, 0)),
                  pl.BlockSpec((2, LANE, LANE), lambda b: (0, 0, 0))],
        out_specs=pl.BlockSpec((nblk, 256), lambda b: (b, 0)),
        out_shape=jax.ShapeDtypeStruct((batch * nblk, 256), F32),
        compiler_params=pltpu.CompilerParams(dimension_semantics=("parallel",), vmem_limit_bytes=VMEM_LIMIT),
        name="nsa_compress",
    )(cmp_k, cmp_v, pos2, w1bd, w2bd)


NSA_TQ = 128
NSA_TK = 256
WIN_SPAN = WINDOW + NSA_TQ


def _nsa_kernel(q_ref, kc_ref, sel_ref, win_ref, oh_ref, g_ref, c2s_ref, eye_ref, o_ref, kaug, qaug, acc_s):
    tq, tk = NSA_TQ, NSA_TK
    rows3 = NSA_H * tq
    qi = pl.program_id(1)
    n_sel = oh_ref.shape[0] // SEL_BLOCK

    @pl.when(qi == 0)
    def _():
        kaug[:, 0:LANE] = sel_ref[:, 0:LANE]
        kaug[:, LANE:2 * LANE] = oh_ref[...]

    q0 = qi * tq
    t3 = q0 + (lax.broadcasted_iota(jnp.int32, (rows3, 1), 0) & (tq - 1))
    t_lane = q0 + lax.broadcasted_iota(jnp.int32, (1, tq), 1)
    gates = jax.nn.sigmoid(g_ref[...])
    outs = []
    for g in range(NSA_G):
        glanes = slice(g * HEAD_DIM, (g + 1) * HEAD_DIM)
        qaug[:, 0:LANE] = jnp.zeros((rows3, LANE), BF16)
        for h in range(NSA_H):
            c = (NSA_H * g + h) * HEAD_DIM
            qaug[h * tq:(h + 1) * tq, glanes] = q_ref[:, c:c + HEAD_DIM]
        qw = qaug[:, 0:LANE]

        s = _nt(qw, kc_ref[:, 0:LANE].astype(BF16))
        ncol = lax.broadcasted_iota(jnp.int32, (1, s.shape[1]), 1)
        cmask = (CMP_STRIDE * ncol + (CMP_BLOCK - 1)) <= t3
        sm = jnp.where(cmask, s, NEG)
        m = jnp.max(sm, axis=1, keepdims=True)
        e = jnp.where(cmask, jnp.exp(sm - m), 0.0)
        p = e / jnp.maximum(jnp.sum(e, axis=1, keepdims=True), 1e-30)
        o_cmp = _dot(p.astype(BF16), kc_ref[:, LANE:2 * LANE].astype(BF16))

        psum = p[0:tq] + p[tq:2 * tq] + p[2 * tq:3 * tq]
        c2s = c2s_ref[...]
        imp = sum(_nt(c2s, term) for term in _split3(psum))
        jb = lax.broadcasted_iota(jnp.int32, (n_sel, 1), 0)
        cur = t_lane // SEL_BLOCK
        imp = jnp.where((jb == 0) | (jb == cur) | (jb == cur - 1), FORCE_SCORE, imp)
        imp = jnp.where(jb > cur, -FORCE_SCORE, imp)
        rank = jnp.zeros(imp.shape, F32)
        for i in range(n_sel):
            ri = imp[i:i + 1, :]
            ahead = (ri > imp) | ((ri == imp) & (jb > i))
            rank = rank + jnp.where(ahead, 1.0, 0.0)
        sel_t = jnp.where(rank < float(min(SEL_TOP_N, n_sel)), 1.0, 0.0).astype(BF16)
        sel_t = jnp.concatenate([sel_t, jnp.zeros((LANE - n_sel, tq), BF16)], axis=0)
        sel = _nt(eye_ref[...], sel_t)
        bias = ((sel - 1.0) * (-SEL_NEG)).astype(BF16)
        for h in range(NSA_H):
            qaug[h * tq:(h + 1) * tq, LANE:2 * LANE] = bias

        acc_s[...] = jnp.zeros(acc_s.shape, F32)

        def body(c, carry):
            m_run, l_run = carry
            k0 = pl.multiple_of(c * tk, tk)
            sc = _nt(qaug[...], kaug[pl.ds(k0, tk), :])
            kpos = k0 + lax.broadcasted_iota(jnp.int32, (1, tk), 1)
            sc = jnp.where(kpos <= t3, sc, NEG)
            m_new = jnp.maximum(m_run, jnp.max(sc, axis=1, keepdims=True))
            alpha = jnp.exp(m_run - m_new)
            pe = jnp.exp(sc - m_new)
            l_new = alpha * l_run + jnp.sum(pe, axis=1, keepdims=True)
            acc_s[...] = alpha * acc_s[...] + _dot(pe.astype(BF16), sel_ref[pl.ds(k0, tk), LANE:2 * LANE])
            return m_new, l_new

        n_chunks = (q0 + tq + tk - 1) // tk
        _, l_fin = lax.fori_loop(0, n_chunks, body,
                                 (jnp.full((rows3, 1), NEG, F32), jnp.zeros((rows3, 1), F32)))
        o_sel = acc_s[...] / l_fin

        ks = pl.multiple_of(jnp.maximum(q0 - WINDOW, 0), tq)
        sw = _nt(qw, win_ref[pl.ds(ks, WIN_SPAN), 0:LANE])
        diff = t3 - (ks + lax.broadcasted_iota(jnp.int32, (1, WIN_SPAN), 1))
        wmask = (diff >= 0) & (diff < WINDOW)
        sw = jnp.where(wmask, sw, NEG)
        pw = jnp.exp(sw - jnp.max(sw, axis=1, keepdims=True))
        lw = jnp.sum(pw, axis=1, keepdims=True)
        o_win = _dot(pw.astype(BF16), win_ref[pl.ds(ks, WIN_SPAN), LANE:2 * LANE]) / lw

        for h in range(NSA_H):
            rs = slice(h * tq, (h + 1) * tq)
            col = (g * NSA_H + h) * 3
            mix = (gates[:, col:col + 1] * o_cmp[rs] + gates[:, col + 1:col + 2] * o_sel[rs]
                   + gates[:, col + 2:col + 3] * o_win[rs])
            outs.append(mix[:, glanes])
    o_ref[...] = jnp.concatenate(outs, axis=1).astype(o_ref.dtype)


def _nsa(q, kc, sel, win, small, onehot, c2s_t, eye, batch, seq):
    nq = seq // NSA_TQ
    nblk = seq // CMP_STRIDE
    qrow = lambda b, i: (b * nq + i, 0)
    per_b = lambda b, i: (b, 0)
    const = lambda b, i: (0, 0)
    return pl.pallas_call(
        _nsa_kernel,
        grid=(batch, nq),
        in_specs=[pl.BlockSpec((NSA_TQ, NSA_WIDTH), qrow),
                  pl.BlockSpec((nblk, 256), per_b),
                  pl.BlockSpec((seq, 256), per_b),
                  pl.BlockSpec((seq, 256), per_b),
                  pl.BlockSpec((seq, LANE), const),
                  pl.BlockSpec((NSA_TQ, LANE), qrow),
                  pl.BlockSpec(c2s_t.shape, const),
                  pl.BlockSpec((NSA_TQ, NSA_TQ), const)],
        out_specs=pl.BlockSpec((NSA_TQ, NSA_WIDTH), qrow),
        out_shape=jax.ShapeDtypeStruct((batch * seq, NSA_WIDTH), BF16),
        scratch_shapes=[pltpu.VMEM((seq, 2 * LANE), BF16),
                        pltpu.VMEM((NSA_H * NSA_TQ, 2 * LANE), BF16),
                        pltpu.VMEM((NSA_H * NSA_TQ, LANE), F32)],
        compiler_params=pltpu.CompilerParams(dimension_semantics=("parallel", "arbitrary"),
                                             vmem_limit_bytes=VMEM_LIMIT),
        name="nsa_attention",
    )(q, kc, sel, win, onehot, small, c2s_t, eye)


GLA_TILE = 256


def _log_sigmoid(x):
    return jnp.minimum(x, 0.0) - jnp.log(1.0 + jnp.exp(-jnp.abs(x)))


def _gla_kernel(q_ref, k_ref, v_ref, og_ref, small_ref, wg_ref, bg_ref, ng_ref, ltri_ref, lsum_ref, bd_ref,
                o_ref, qd_s, ki_s, ke_s, dec_s, oacc_s, st_s):
    seq = q_ref.shape[0]
    ch = GLA_CHUNK
    nh = GLA_HEADS

    def pre(i, _):
        rs = pl.ds(pl.multiple_of(i * GLA_TILE, GLA_TILE), GLA_TILE)
        x_hi, x_mid, _ = _split3(small_ref[rs, :])
        w_hi, w_mid, _ = _split3(wg_ref[...])
        z = _dot(x_hi, w_hi) + (_dot(x_hi, w_mid) + _dot(x_mid, w_hi)) + bg_ref[...]
        gl = _log_sigmoid(z) * (1.0 / GLA_GATE_NORM)
        parts = _split3(gl)
        bc = sum(_dot(ltri_ref[...], term) for term in parts)
        bl = sum(_dot(lsum_ref[...], term) for term in parts)
        qf = q_ref[rs, :].astype(F32) * (GLA_DK ** -0.5)
        kf = k_ref[rs, :].astype(F32)
        qd_s[rs, :] = (qf * jnp.exp(bc)).astype(BF16)
        ki_s[rs, :] = (kf * jnp.exp(-bc)).astype(BF16)
        ke_s[rs, :] = (kf * jnp.exp(bl - bc)).astype(BF16)
        dec_s[rs, :] = jnp.exp(bl)
        return 0

    lax.fori_loop(0, seq // GLA_TILE, pre, 0)

    qk_head = lax.broadcasted_iota(jnp.int32, (1, GLA_QK_PAD), 1) // GLA_DK
    v_head = lax.broadcasted_iota(jnp.int32, (1, GLA_WIDTH), 1) // GLA_DV
    st_rows_head = lax.broadcasted_iota(jnp.int32, (GLA_WIDTH, 1), 0) // GLA_DV
    st_mask = st_rows_head == qk_head
    ri = lax.broadcasted_iota(jnp.int32, (nh * ch, 1), 0) & (ch - 1)
    tril = lax.broadcasted_iota(jnp.int32, (1, ch), 1) <= ri
    st_s[...] = jnp.zeros(st_s.shape, F32)

    def step(n, _):
        rs = pl.ds(pl.multiple_of(n * ch, ch), ch)
        qd = qd_s[rs, :]
        v = v_ref[rs, :]
        qrep = jnp.concatenate([jnp.where(qk_head == h, qd, jnp.zeros_like(qd)) for h in range(nh)], axis=0)
        a = jnp.where(tril, _nt(qrep, ki_s[rs, :]), 0.0)
        r = _dot(a.astype(BF16), v)
        o = _nt(qd, st_s[...].astype(BF16))
        for h in range(nh):
            o = o + jnp.where(v_head == h, r[h * ch:(h + 1) * ch], 0.0)
        oacc_s[rs, :] = o
        dt = _tn(v, ke_s[rs, :])
        st_s[...] = st_s[...] * dec_s[pl.ds(pl.multiple_of(n * ch, ch), 1), :] + jnp.where(st_mask, dt, 0.0)
        return 0

    lax.fori_loop(0, seq // ch, step, 0)

    def post(i, _):
        rs = pl.ds(pl.multiple_of(i * GLA_TILE, GLA_TILE), GLA_TILE)
        o = oacc_s[rs, :]
        ms = sum(_dot(term, bd_ref[...]) for term in _split3(o * o)[:2])
        og = og_ref[rs, :].astype(F32)
        y = o * lax.rsqrt(ms + RMS_EPS) * ng_ref[...] * (og * jax.nn.sigmoid(og))
        o_ref[rs, :] = y.astype(o_ref.dtype)
        return 0

    lax.fori_loop(0, seq // GLA_TILE, post, 0)


def _gla(gq, gk, gv, og, small, wg, bg, ng, ltri, lsum, bd, batch, seq):
    per_b = lambda b: (b, 0)
    const = lambda b: (0, 0)
    return pl.pallas_call(
        _gla_kernel,
        grid=(batch,),
        in_specs=[pl.BlockSpec((seq, GLA_QK_PAD), per_b), pl.BlockSpec((seq, GLA_QK_PAD), per_b),
                  pl.BlockSpec((seq, GLA_WIDTH), per_b), pl.BlockSpec((seq, GLA_WIDTH), per_b),
                  pl.BlockSpec((seq, LANE), per_b),
                  pl.BlockSpec(wg.shape, const), pl.BlockSpec(bg.shape, const), pl.BlockSpec(ng.shape, const),
                  pl.BlockSpec(ltri.shape, const), pl.BlockSpec(lsum.shape, const), pl.BlockSpec(bd.shape, const)],
        out_specs=pl.BlockSpec((seq, GLA_WIDTH), per_b),
        out_shape=jax.ShapeDtypeStruct((batch * seq, GLA_WIDTH), BF16),
        scratch_shapes=[pltpu.VMEM((seq, GLA_QK_PAD), BF16), pltpu.VMEM((seq, GLA_QK_PAD), BF16),
                        pltpu.VMEM((seq, GLA_QK_PAD), BF16), pltpu.VMEM((seq, GLA_QK_PAD), F32),
                        pltpu.VMEM((seq, GLA_WIDTH), F32), pltpu.VMEM((GLA_WIDTH, GLA_QK_PAD), F32)],
        compiler_params=pltpu.CompilerParams(dimension_semantics=("parallel",), vmem_limit_bytes=VMEM_LIMIT),
        name="gla",
    )(gq, gk, gv, og, small, wg, bg, ng, ltri, lsum, bd)


POOL_TILE = 256


def _pool_kernel(u_ref, band_cur_ref, band_prev_ref, w_ref, scale_ref, o_ref):
    seq = u_ref.shape[0]
    lane = lax.broadcasted_iota(jnp.int32, (1, POOL_WIDTH), 1)
    grp = lane // POOL_GROUP
    win = jnp.where(grp == 0, POOL_WINDOWS[0],
                    jnp.where(grp == 1, POOL_WINDOWS[1], jnp.where(grp == 2, POOL_WINDOWS[2], POOL_WINDOWS[3])))
    for i in range(seq // POOL_TILE):
        rs = slice(i * POOL_TILE, (i + 1) * POOL_TILE)
        u = u_ref[rs, :]
        tot = jnp.zeros((POOL_TILE, POOL_WIDTH), F32)
        for gi in range(len(POOL_WINDOWS)):
            s = _dot(band_cur_ref[gi], u)
            if i > 0:
                s = s + _dot(band_prev_ref[gi], u_ref[(i - 1) * POOL_TILE:i * POOL_TILE, :])
            tot = jnp.where(grp == gi, s, tot)
        t = i * POOL_TILE + lax.broadcasted_iota(jnp.int32, (POOL_TILE, 1), 0)
        cnt = jnp.minimum(t + 1, win).astype(F32)
        pooled = tot / cnt - u.astype(F32)
        o_ref[rs, :] = (_dot(pooled.astype(BF16), w_ref[...]) * scale_ref[...]).astype(o_ref.dtype)


def _pool(u, band_cur, band_prev, wbd, scale, batch, seq):
    per_b = lambda b: (b, 0)
    return pl.pallas_call(
        _pool_kernel,
        grid=(batch,),
        in_specs=[pl.BlockSpec((seq, POOL_WIDTH), per_b),
                  pl.BlockSpec(band_cur.shape, lambda b: (0, 0, 0)),
                  pl.BlockSpec(band_prev.shape, lambda b: (0, 0, 0)),
                  pl.BlockSpec(wbd.shape, lambda b: (0, 0)),
                  pl.BlockSpec(scale.shape, lambda b: (0, 0))],
        out_specs=pl.BlockSpec((seq, POOL_WIDTH), per_b),
        out_shape=jax.ShapeDtypeStruct((batch * seq, POOL_WIDTH), BF16),
        compiler_params=pltpu.CompilerParams(dimension_semantics=("parallel",), vmem_limit_bytes=VMEM_LIMIT),
        name="pool",
    )(u, band_cur, band_prev, wbd, scale)


def _oproj_kernel(nsa_ref, gla_ref, pool_ref, h_ref, w_ref, g_ref, b_ref, o_ref):
    acc = _dot(nsa_ref[...], w_ref[0:NSA_WIDTH, :])
    acc = acc + _dot(gla_ref[...], w_ref[NSA_WIDTH:NSA_WIDTH + GLA_WIDTH, :])
    acc = acc + _dot(pool_ref[...], w_ref[NSA_WIDTH + GLA_WIDTH:, :])
    o_ref[...] = _layer_norm(ALPHA * h_ref[...] + acc, g_ref[...], b_ref[...])


def _oproj(nsa, gla, pool, h, w, g, b, tm):
    m = h.shape[0]
    row = lambda i: (i, 0)
    const = lambda i: (0, 0)
    return pl.pallas_call(
        _oproj_kernel,
        grid=(m // tm,),
        in_specs=[pl.BlockSpec((tm, NSA_WIDTH), row), pl.BlockSpec((tm, GLA_WIDTH), row),
                  pl.BlockSpec((tm, POOL_WIDTH), row), pl.BlockSpec((tm, D_MODEL), row),
                  pl.BlockSpec((D_MODEL, D_MODEL), const, pipeline_mode=pl.Buffered(1)),
                  pl.BlockSpec((1, D_MODEL), const), pl.BlockSpec((1, D_MODEL), const)],
        out_specs=pl.BlockSpec((tm, D_MODEL), row),
        out_shape=jax.ShapeDtypeStruct((m, D_MODEL), F32),
        compiler_params=pltpu.CompilerParams(dimension_semantics=("parallel",), vmem_limit_bytes=VMEM_LIMIT),
        name="out_proj_ln",
    )(nsa, gla, pool, h, w, g, b)


FFN_CHUNK = 1024


def _ffn_kernel(h_ref, wu_ref, wd_ref, g_ref, b_ref, o_ref):
    h = h_ref[...]
    hb = h.astype(BF16)
    acc = ALPHA * h
    for c in range(D_FF // FFN_CHUNK):
        cs = slice(c * FFN_CHUNK, (c + 1) * FFN_CHUNK)
        u = jnp.maximum(_dot(hb, wu_ref[:, cs]), 0.0)
        acc = acc + _dot((u * u).astype(BF16), wd_ref[cs, :])
    o_ref[...] = _layer_norm(acc, g_ref[...], b_ref[...])


def _ffn(h, wu, wd, g, b, tm):
    m = h.shape[0]
    row = lambda i: (i, 0)
    const = lambda i: (0, 0)
    return pl.pallas_call(
        _ffn_kernel,
        grid=(m // tm,),
        in_specs=[pl.BlockSpec((tm, D_MODEL), row),
                  pl.BlockSpec((D_MODEL, D_FF), const, pipeline_mode=pl.Buffered(1)),
                  pl.BlockSpec((D_FF, D_MODEL), const, pipeline_mode=pl.Buffered(1)),
                  pl.BlockSpec((1, D_MODEL), const), pl.BlockSpec((1, D_MODEL), const)],
        out_specs=pl.BlockSpec((tm, D_MODEL), row),
        out_shape=jax.ShapeDtypeStruct((m, D_MODEL), F32),
        compiler_params=pltpu.CompilerParams(dimension_semantics=("parallel",), vmem_limit_bytes=VMEM_LIMIT),
        name="ffn_ln",
    )(h, wu, wd, g, b)


def _rope_tables(seq, scale):
    half = ROPE_DIM // 2
    inv = ROPE_THETA ** (-jnp.arange(half, dtype=F32) * 2.0 / ROPE_DIM)
    ang = jnp.arange(seq).astype(F32)[:, None] * inv[None, :]
    cos, sin = jnp.cos(ang), jnp.sin(ang)
    ones = jnp.ones((seq, HEAD_DIM - ROPE_DIM), F32)
    zeros = jnp.zeros((seq, HEAD_DIM - ROPE_DIM), F32)
    zh = jnp.zeros((seq, half), F32)
    a = jnp.concatenate([cos, cos, ones], axis=1)
    bm = jnp.concatenate([-sin, zh, zeros], axis=1)
    cm = jnp.concatenate([zh, sin, zeros], axis=1)
    reps = LANE // HEAD_DIM
    return tuple(jnp.tile(t, (1, reps)) * scale for t in (a, bm, cm))


def _static_tables(seq):
    n_sel = seq // SEL_BLOCK
    nblk = seq // CMP_STRIDE
    n_cmp = (seq - CMP_BLOCK) // CMP_STRIDE + 1
    onehot = np.zeros((seq, LANE), np.float32)
    onehot[np.arange(seq), np.arange(seq) // SEL_BLOCK] = 1.0
    c_lo = CMP_STRIDE * np.arange(n_cmp)[:, None]
    s_lo = SEL_BLOCK * np.arange(n_sel)[None, :]
    overlap = np.clip(np.minimum(c_lo + CMP_BLOCK, s_lo + SEL_BLOCK) - np.maximum(c_lo, s_lo), 0, None)
    c2s_t = np.zeros((n_sel, nblk), np.float32)
    c2s_t[:, :n_cmp] = (overlap.astype(np.float32) / CMP_BLOCK).T
    eye = np.eye(NSA_TQ, dtype=np.float32)
    r = np.arange(GLA_TILE)
    same = (r[:, None] // GLA_CHUNK) == (r[None, :] // GLA_CHUNK)
    ltri = (same & (r[None, :] <= r[:, None])).astype(np.float32)
    lsum = same.astype(np.float32)
    hv = np.arange(GLA_WIDTH) // GLA_DV
    bd = (hv[:, None] == hv[None, :]).astype(np.float32) / GLA_DV
    band_cur = np.zeros((len(POOL_WINDOWS), POOL_TILE, POOL_TILE), np.float32)
    band_prev = np.zeros_like(band_cur)
    pi = np.arange(POOL_TILE)
    for gi, w in enumerate(POOL_WINDOWS):
        d = pi[:, None] - pi[None, :]
        band_cur[gi] = ((d >= 0) & (d < w)).astype(np.float32)
        band_prev[gi] = ((d + POOL_TILE >= 0) & (d + POOL_TILE < w)).astype(np.float32)
    as_bf = lambda a: jnp.asarray(a, BF16)
    return dict(onehot=as_bf(onehot), c2s_t=as_bf(c2s_t), eye=as_bf(eye), ltri=as_bf(ltri), lsum=as_bf(lsum),
                bd=as_bf(bd), band_cur=as_bf(band_cur), band_prev=as_bf(band_prev))


def _block_diag2(w):
    z = jnp.zeros_like(w)
    return jnp.concatenate([jnp.concatenate([w, z], axis=-1), jnp.concatenate([z, w], axis=-1)], axis=-2)


def _layer_weights(w_in, cmp_pos, cmp_w1, cmp_w2, w_gate2, b_gate, norm_g, pool_w, pool_scale):
    o_nq, o_nkv, o_gate = 0, NSA_WIDTH, NSA_WIDTH + 768
    o_gq = o_gate + 18
    o_gk = o_gq + 192
    o_gv = o_gk + 192
    o_glr = o_gv + GLA_WIDTH
    o_og = o_glr + GLA_RANK
    o_pu = o_og + GLA_WIDTH
    zc = lambda n: jnp.zeros((D_MODEL, n), F32)
    small = jnp.concatenate([w_in[:, o_gate:o_gq], zc(SMALL_GLR0 - 18), w_in[:, o_glr:o_og],
                             zc(LANE - SMALL_GLR0 - GLA_RANK)], axis=1)
    wp = jnp.concatenate([w_in[:, o_nq:o_gate], small,
                          w_in[:, o_gq:o_gk], zc(GLA_QK_PAD - 192), w_in[:, o_gk:o_gv], zc(GLA_QK_PAD - 192),
                          w_in[:, o_gv:o_glr], w_in[:, o_og:o_pu], w_in[:, o_pu:]], axis=1).astype(BF16)
    pos2 = jnp.concatenate([cmp_pos, cmp_pos], axis=-1)
    w1bd = _block_diag2(cmp_w1.reshape(2, CMP_BLOCK, HEAD_DIM, cmp_w1.shape[-1])).astype(BF16)
    w2bd = _block_diag2(cmp_w2).astype(BF16)
    wg = jnp.zeros((LANE, GLA_QK_PAD), F32).at[SMALL_GLR0:SMALL_GLR0 + GLA_RANK, :192].set(w_gate2)
    bg = jnp.zeros((1, GLA_QK_PAD), F32).at[0, :192].set(b_gate)
    ng = norm_g.reshape(1, GLA_WIDTH)
    pw = jnp.zeros((POOL_WIDTH, POOL_WIDTH), F32)
    for gi in range(len(POOL_WINDOWS)):
        sl = slice(gi * POOL_GROUP, (gi + 1) * POOL_GROUP)
        pw = pw.at[sl, sl].set(pool_w[gi])
    return dict(wp=wp, pos2=pos2, w1bd=w1bd, w2bd=w2bd, wg=wg, bg=bg, ng=ng, pw=pw.astype(BF16),
                ps=pool_scale.reshape(1, POOL_WIDTH))


@jax.jit
def kernel(x, w_in, cmp_pos, cmp_w1, cmp_w2, gla_w_gate2, gla_b_gate, gla_norm_g, pool_w, pool_scale,
           w_out, ln1_g, ln1_b, w_up, w_down, ln2_g, ln2_b):
    batch, seq, d = x.shape
    assert d == D_MODEL and seq % GLA_TILE == 0 and seq >= WIN_SPAN and seq // SEL_BLOCK <= LANE
    tm = 512
    st = _static_tables(seq)
    rope_q = _rope_tables(seq, HEAD_DIM ** -0.5)
    rope_k = _rope_tables(seq, 1.0)
    h = x.reshape(batch * seq, d)
    for l in range(w_in.shape[0]):
        lw = _layer_weights(w_in[l], cmp_pos[l], cmp_w1[l], cmp_w2[l], gla_w_gate2[l], gla_b_gate[l],
                            gla_norm_g[l], pool_w[l], pool_scale[l])
        q, cmp_k, cmp_v, sel, win, small, gq, gk, gv, og, pu = _proj(h, lw["wp"], rope_q, rope_k, seq, tm)
        kc = _compress(cmp_k, cmp_v, lw["pos2"], lw["w1bd"], lw["w2bd"], batch, seq)
        nsa = _nsa(q, kc, sel, win, small, st["onehot"], st["c2s_t"], st["eye"], batch, seq)
        gla = _gla(gq, gk, gv, og, small, lw["wg"], lw["bg"], lw["ng"], st["ltri"], st["lsum"], st["bd"],
                   batch, seq)
        pool = _pool(pu, st["band_cur"], st["band_prev"], lw["pw"], lw["ps"], batch, seq)
        h = _oproj(nsa, gla, pool, h, w_out[l].astype(BF16), ln1_g[l].reshape(1, d), ln1_b[l].reshape(1, d), tm)
        h = _ffn(h, w_up[l].astype(BF16), w_down[l].astype(BF16), ln2_g[l].reshape(1, d), ln2_b[l].reshape(1, d), tm)
    return h.reshape(batch, seq, d)
```

```python
import functools
import numpy as np
import jax
import jax.numpy as jnp
from jax import lax
from jax.experimental import pallas as pl
from jax.experimental.pallas import tpu as pltpu

F32 = jnp.float32
BF16 = jnp.bfloat16

D_MODEL = 1024
HEAD_DIM = 64
NSA_G = 2
NSA_H = 3
NSA_WIDTH = NSA_G * NSA_H * HEAD_DIM
CMP_BLOCK = 32
CMP_STRIDE = 16
SEL_BLOCK = 64
SEL_TOP_N = 16
WINDOW = 512
FORCE_SCORE = 1.0e4
GLA_HEADS = 6
GLA_DK = 32
GLA_DV = 64
GLA_WIDTH = GLA_HEADS * GLA_DV
GLA_RANK = 16
GLA_GATE_NORM = 16.0
GLA_CHUNK = 64
GLA_QK_PAD = 256
POOL_WINDOWS = (2, 4, 8, 16)
POOL_GROUP = 64
POOL_WIDTH = 256
D_FF = 4 * D_MODEL
ROPE_THETA = 500000.0
ROPE_DIM = HEAD_DIM // 4
LN_EPS = 1e-5
RMS_EPS = 1e-6
DEPTH = 4
ALPHA = (2 * DEPTH) ** 0.25

LANE = 128
NEG = -1e30
SEL_NEG = -30000.0

VMEM_LIMIT = 48 * 1024 * 1024

SMALL_GATE0 = 0
SMALL_GLR0 = 32
PROJ_COLS = 22 * LANE


def _nt(a, b):
    return lax.dot_general(a, b, (((1,), (1,)), ((), ())), preferred_element_type=F32)


def _tn(a, b):
    return lax.dot_general(a, b, (((0,), (0,)), ((), ())), preferred_element_type=F32)


def _dot(a, b):
    return jnp.dot(a, b, preferred_element_type=F32)


def _split3(x):
    hi = x.astype(BF16)
    r = x - hi.astype(F32)
    mid = r.astype(BF16)
    lo = (r - mid.astype(F32)).astype(BF16)
    return hi, mid, lo


def _layer_norm(y, g, b):
    mu = jnp.mean(y, axis=-1, keepdims=True)
    d = y - mu
    var = jnp.mean(d * d, axis=-1, keepdims=True)
    return d * lax.rsqrt(var + LN_EPS) * g + b


def _rope(x, a, bm, cm):
    return x * a + pltpu.roll(x, LANE - ROPE_DIM // 2, 1) * bm + pltpu.roll(x, ROPE_DIM // 2, 1) * cm


def _proj_kernel(h_ref, w_ref, qa_ref, qb_ref, qc_ref, ka_ref, kb_ref, kc_ref,
                 q_out, cmpk_out, cmpv_out, sel_out, win_out, small_out, gq_out, gk_out, gv_out, og_out, pu_out):
    hb = h_ref[...].astype(BF16)
    plan = ([(q_out, j, "q") for j in range(3)]
            + [(cmpk_out, 0, "k"), (cmpv_out, 0, None), (sel_out, 0, "k"), (sel_out, 1, None),
               (win_out, 0, "k"), (win_out, 1, None), (small_out, 0, None)]
            + [(gq_out, j, None) for j in range(2)] + [(gk_out, j, None) for j in range(2)]
            + [(gv_out, j, None) for j in range(3)] + [(og_out, j, None) for j in range(3)]
            + [(pu_out, j, None) for j in range(2)])
    for c in range(0, len(plan), 2):
        acc = _dot(hb, w_ref[:, c * LANE:(c + 2) * LANE])
        for half in range(2):
            ref, j, kind = plan[c + half]
            x = acc[:, half * LANE:(half + 1) * LANE]
            if kind == "q":
                x = _rope(x, qa_ref[...], qb_ref[...], qc_ref[...])
            elif kind == "k":
                x = _rope(x, ka_ref[...], kb_ref[...], kc_ref[...])
            ref[:, j * LANE:(j + 1) * LANE] = x.astype(ref.dtype)


def _proj(h, w, rope_q, rope_k, seq, tm):
    m = h.shape[0]
    tiles_per_seq = seq // tm
    row = lambda i: (i, 0)
    tab = lambda i: (i % tiles_per_seq, 0)
    widths = [(NSA_WIDTH, BF16), (LANE, F32), (LANE, F32), (256, BF16), (256, BF16), (LANE, F32),
              (GLA_QK_PAD, BF16), (GLA_QK_PAD, BF16), (GLA_WIDTH, BF16), (GLA_WIDTH, BF16), (POOL_WIDTH, BF16)]
    return pl.pallas_call(
        _proj_kernel,
        grid=(m // tm,),
        in_specs=[pl.BlockSpec((tm, D_MODEL), row),
                  pl.BlockSpec((D_MODEL, PROJ_COLS), lambda i: (0, 0), pipeline_mode=pl.Buffered(1))]
                 + [pl.BlockSpec((tm, LANE), tab)] * 6,
        out_specs=[pl.BlockSpec((tm, wd), row) for wd, _ in widths],
        out_shape=[jax.ShapeDtypeStruct((m, wd), dt) for wd, dt in widths],
        compiler_params=pltpu.CompilerParams(dimension_semantics=("parallel",), vmem_limit_bytes=VMEM_LIMIT),
        name="in_proj",
    )(h, w, *rope_q, *rope_k)


def _cmp_kernel(k_ref, v_ref, pos_ref, w1_ref, w2_ref, o_ref):
    nblk = o_ref.shape[0]
    half = CMP_BLOCK // 2
    for kvi, x_ref in enumerate((k_ref, v_ref)):
        lanes = slice(kvi * LANE, (kvi + 1) * LANE)
        ha = jnp.zeros((nblk, LANE), F32)
        hb = jnp.zeros((nblk, LANE), F32)
        for r in range(half):
            t = x_ref[pl.ds(r, nblk, stride=CMP_STRIDE), :]
            ha = ha + _dot((t + pos_ref[kvi, r:r + 1, :]).astype(BF16), w1_ref[kvi, r])
            hb = hb + _dot((t + pos_ref[kvi, half + r:half + r + 1, :]).astype(BF16), w1_ref[kvi, half + r])
        hid = ha + pltpu.roll(hb, nblk - 1, 0)
        hid = jax.nn.gelu(hid, approximate=True)
        o_ref[:, lanes] = _dot(hid.astype(BF16), w2_ref[kvi])


def _compress(cmp_k, cmp_v, pos2, w1bd, w2bd, batch, seq):
    nblk = seq // CMP_STRIDE
    return pl.pallas_call(
        _cmp_kernel,
        grid=(batch,),
        in_specs=[pl.BlockSpec((seq, LANE), lambda b: (b, 0)), pl.BlockSpec((seq, LANE), lambda b: (b, 0)),
                  pl.BlockSpec((2, CMP_BLOCK, LANE), lambda b: (0, 0, 0)),
                  pl.BlockSpec((2, CMP_BLOCK, LANE, LANE), lambda b: (0, 0, 0, 0)),
                  pl.BlockSpec((2, LANE, LANE), lambda b: (0, 0, 0))],
        out_specs=pl.BlockSpec((nblk, 256), lambda b: (b, 0)),
        out_shape=jax.ShapeDtypeStruct((batch * nblk, 256), F32),
        compiler_params=pltpu.CompilerParams(dimension_semantics=("parallel",), vmem_limit_bytes=VMEM_LIMIT),
        name="nsa_compress",
    )(cmp_k, cmp_v, pos2, w1bd, w2bd)


NSA_TQ = 256
NSA_TK = 512
WIN_SPAN = WINDOW + NSA_TQ
LOG2E = 1.4426950408889634


def _nsa_kernel(q_ref, kc_ref, sel_ref, win_ref, oh_ref, g_ref, c2s_ref, eye_ref, perm_ref, o_ref,
                kaug, vsel, vwin, vcmp, qaug, mix_scr):
    tq, tk = NSA_TQ, NSA_TK
    rows3 = NSA_H * tq
    qi = pl.program_id(1)
    n_sel = oh_ref.shape[0] // SEL_BLOCK
    lane = lax.broadcasted_iota(jnp.int32, (1, LANE), 1)
    ones_lane = jnp.where(lane == HEAD_DIM, 1.0, 0.0)

    @pl.when(qi == 0)
    def _():
        kaug[:, 0:LANE] = sel_ref[:, 0:LANE]
        kaug[:, LANE:2 * LANE] = oh_ref[...]
        for g in range(NSA_G):
            vsel[g] = (_dot(sel_ref[:, LANE:2 * LANE], perm_ref[g]) + ones_lane).astype(BF16)
            vwin[g] = (_dot(win_ref[:, LANE:2 * LANE], perm_ref[g]) + ones_lane).astype(BF16)
            vcmp[g] = _dot(kc_ref[:, LANE:2 * LANE].astype(BF16), perm_ref[g]).astype(BF16)

    q0 = qi * tq
    t3 = q0 + (lax.broadcasted_iota(jnp.int32, (rows3, 1), 0) & (tq - 1))
    t_lane = q0 + lax.broadcasted_iota(jnp.int32, (1, tq), 1)
    gates = jax.nn.sigmoid(g_ref[...])
    kc_b = kc_ref[:, 0:LANE].astype(BF16)
    for g in range(NSA_G):
        glanes = slice(g * HEAD_DIM, (g + 1) * HEAD_DIM)
        qaug[g, :, 0:LANE] = jnp.zeros((rows3, LANE), BF16)
        for h in range(NSA_H):
            c = (NSA_H * g + h) * HEAD_DIM
            qaug[g, h * tq:(h + 1) * tq, glanes] = q_ref[:, c:c + HEAD_DIM]

    ks = pl.multiple_of(jnp.maximum(q0 - WINDOW, 0), tq)
    kw = win_ref[pl.ds(ks, WIN_SPAN), 0:LANE]
    diff = t3 - (ks + lax.broadcasted_iota(jnp.int32, (1, WIN_SPAN), 1))
    wmask = (diff >= 0) & (diff < WINDOW)
    o_win = []
    for g in range(NSA_G):
        sw = jnp.where(wmask, _nt(qaug[g, :, 0:LANE], kw), NEG)
        pw = jnp.exp2(sw - jnp.max(sw, axis=1, keepdims=True)).astype(BF16)
        a_win = _dot(pw, vwin[g, pl.ds(ks, WIN_SPAN), :])
        o_win.append(a_win * (1.0 / a_win[:, HEAD_DIM:HEAD_DIM + 1]))

    for g in range(NSA_G):
        s = _nt(qaug[g, :, 0:LANE], kc_b)
        ncol = lax.broadcasted_iota(jnp.int32, (1, s.shape[1]), 1)
        cmask = (CMP_STRIDE * ncol + (CMP_BLOCK - 1)) <= t3
        sm = jnp.where(cmask, s, NEG)
        m = jnp.max(sm, axis=1, keepdims=True)
        e = jnp.where(cmask, jnp.exp2(sm - m), 0.0)
        p = e / jnp.maximum(jnp.sum(e, axis=1, keepdims=True), 1e-30)
        o_cmp = _dot(p.astype(BF16), vcmp[g])
        for h in range(NSA_H):
            rs = slice(h * tq, (h + 1) * tq)
            col = (g * NSA_H + h) * 3
            mix_scr[g, rs] = gates[:, col:col + 1] * o_cmp[rs] + gates[:, col + 2:col + 3] * o_win[g][rs]

        psum = p[0:tq] + p[tq:2 * tq] + p[2 * tq:3 * tq]
        c2s = c2s_ref[...]
        imp = sum(_nt(c2s, term) for term in _split3(psum))
        jb = lax.broadcasted_iota(jnp.int32, (n_sel, 1), 0)
        cur = t_lane // SEL_BLOCK
        imp = jnp.where((jb == 0) | (jb == cur) | (jb == cur - 1), FORCE_SCORE, imp)
        imp = jnp.where(jb > cur, -FORCE_SCORE, imp)
        rank = jnp.zeros(imp.shape, F32)
        for i in range(n_sel):
            ri = imp[i:i + 1, :]
            ahead = (ri > imp) | ((ri == imp) & (jb > i))
            rank = rank + jnp.where(ahead, 1.0, 0.0)
        sel_t = jnp.where(rank < float(min(SEL_TOP_N, n_sel)), 1.0, 0.0).astype(BF16)
        sel_t = jnp.concatenate([sel_t, jnp.zeros((LANE - n_sel, tq), BF16)], axis=0)
        sel = _nt(eye_ref[...], sel_t)
        bias = ((sel - 1.0) * (-SEL_NEG)).astype(BF16)
        for h in range(NSA_H):
            qaug[g, h * tq:(h + 1) * tq, LANE:2 * LANE] = bias

    last = (q0 + tq - 1) // tk

    def selected(n_chunks):
        lo = (n_chunks - 1) * tk
        kpos = lo + lax.broadcasted_iota(jnp.int32, (1, tk), 1)
        outs = []
        for g in range(NSA_G):
            qa = qaug[g]
            sb = jnp.where(kpos <= t3, _nt(qa, kaug[lo:lo + tk, :]), NEG)
            m = jnp.max(sb, axis=1, keepdims=True)
            if lo:
                sa = _nt(qa, kaug[0:lo, :])
                m = jnp.maximum(m, jnp.max(sa, axis=1, keepdims=True))
            acc = _dot(jnp.exp2(sb - m).astype(BF16), vsel[g, lo:lo + tk, :])
            if lo:
                acc = acc + _dot(jnp.exp2(sa - m).astype(BF16), vsel[g, 0:lo, :])
            o_sel = acc * (1.0 / acc[:, HEAD_DIM:HEAD_DIM + 1])
            for h in range(NSA_H):
                rs = slice(h * tq, (h + 1) * tq)
                col = (g * NSA_H + h) * 3
                mix = mix_scr[g, rs] + gates[:, col + 1:col + 2] * o_sel[rs]
                outs.append(mix[:, 0:HEAD_DIM])
        o_ref[...] = jnp.concatenate(outs, axis=1).astype(o_ref.dtype)

    for n_chunks in range(1, kaug.shape[0] // tk + 1):
        pl.when(last == n_chunks - 1)(functools.partial(selected, n_chunks))


def _nsa(q, kc, sel, win, small, onehot, c2s_t, eye, perm, batch, seq):
    nq = seq // NSA_TQ
    nblk = seq // CMP_STRIDE
    rows3 = NSA_H * NSA_TQ
    qrow = lambda b, i: (b * nq + i, 0)
    per_b = lambda b, i: (b, 0)
    const = lambda b, i: (0, 0)
    return pl.pallas_call(
        _nsa_kernel,
        grid=(batch, nq),
        in_specs=[pl.BlockSpec((NSA_TQ, NSA_WIDTH), qrow),
                  pl.BlockSpec((nblk, 256), per_b),
                  pl.BlockSpec((seq, 256), per_b),
                  pl.BlockSpec((seq, 256), per_b),
                  pl.BlockSpec((seq, LANE), const),
                  pl.BlockSpec((NSA_TQ, LANE), qrow),
                  pl.BlockSpec(c2s_t.shape, const),
                  pl.BlockSpec((NSA_TQ, NSA_TQ), const),
                  pl.BlockSpec(perm.shape, lambda b, i: (0, 0, 0))],
        out_specs=pl.BlockSpec((NSA_TQ, NSA_WIDTH), qrow),
        out_shape=jax.ShapeDtypeStruct((batch * seq, NSA_WIDTH), BF16),
        scratch_shapes=[pltpu.VMEM((seq, 2 * LANE), BF16),
                        pltpu.VMEM((NSA_G, seq, LANE), BF16),
                        pltpu.VMEM((NSA_G, seq, LANE), BF16),
                        pltpu.VMEM((NSA_G, nblk, LANE), BF16),
                        pltpu.VMEM((NSA_G, rows3, 2 * LANE), BF16),
                        pltpu.VMEM((NSA_G, rows3, LANE), F32)],
        compiler_params=pltpu.CompilerParams(dimension_semantics=("parallel", "arbitrary"),
                                             vmem_limit_bytes=VMEM_LIMIT),
        name="nsa_attention",
    )(q, kc, sel, win, onehot, small, c2s_t, eye, perm)


GLA_TILE = 256


def _log_sigmoid(x):
    return jnp.minimum(x, 0.0) - jnp.log(1.0 + jnp.exp(-jnp.abs(x)))


def _gla_kernel(q_ref, k_ref, v_ref, og_ref, small_ref, wg_ref, bg_ref, ng_ref, ltri_ref, lsum_ref, bd_ref,
                o_ref, qd_s, ki_s, ke_s, dec_s, oacc_s, st_s):
    seq = q_ref.shape[0]
    ch = GLA_CHUNK
    nh = GLA_HEADS

    def pre(i, _):
        rs = pl.ds(pl.multiple_of(i * GLA_TILE, GLA_TILE), GLA_TILE)
        x_hi, x_mid, _ = _split3(small_ref[rs, :])
        w_hi, w_mid, _ = _split3(wg_ref[...])
        z = _dot(x_hi, w_hi) + (_dot(x_hi, w_mid) + _dot(x_mid, w_hi)) + bg_ref[...]
        gl = _log_sigmoid(z) * (1.0 / GLA_GATE_NORM)
        parts = _split3(gl)
        bc = sum(_dot(ltri_ref[...], term) for term in parts)
        bl = sum(_dot(lsum_ref[...], term) for term in parts)
        qf = q_ref[rs, :].astype(F32) * (GLA_DK ** -0.5)
        kf = k_ref[rs, :].astype(F32)
        qd_s[rs, :] = (qf * jnp.exp(bc)).astype(BF16)
        ki_s[rs, :] = (kf * jnp.exp(-bc)).astype(BF16)
        ke_s[rs, :] = (kf * jnp.exp(bl - bc)).astype(BF16)
        dec_s[rs, :] = jnp.exp(bl)
        return 0

    lax.fori_loop(0, seq // GLA_TILE, pre, 0)

    qk_head = lax.broadcasted_iota(jnp.int32, (1, GLA_QK_PAD), 1) // GLA_DK
    v_head = lax.broadcasted_iota(jnp.int32, (1, GLA_WIDTH), 1) // GLA_DV
    st_rows_head = lax.broadcasted_iota(jnp.int32, (GLA_WIDTH, 1), 0) // GLA_DV
    st_mask = st_rows_head == qk_head
    ri = lax.broadcasted_iota(jnp.int32, (nh * ch, 1), 0) & (ch - 1)
    tril = lax.broadcasted_iota(jnp.int32, (1, ch), 1) <= ri
    st_s[...] = jnp.zeros(st_s.shape, F32)

    def step(n, _):
        rs = pl.ds(pl.multiple_of(n * ch, ch), ch)
        qd = qd_s[rs, :]
        v = v_ref[rs, :]
        qrep = jnp.concatenate([jnp.where(qk_head == h, qd, jnp.zeros_like(qd)) for h in range(nh)], axis=0)
        a = jnp.where(tril, _nt(qrep, ki_s[rs, :]), 0.0)
        r = _dot(a.astype(BF16), v)
        o = _nt(qd, st_s[...].astype(BF16))
        for h in range(nh):
            o = o + jnp.where(v_head == h, r[h * ch:(h + 1) * ch], 0.0)
        oacc_s[rs, :] = o
        dt = _tn(v, ke_s[rs, :])
        st_s[...] = st_s[...] * dec_s[pl.ds(pl.multiple_of(n * ch, ch), 1), :] + jnp.where(st_mask, dt, 0.0)
        return 0

    lax.fori_loop(0, seq // ch, step, 0)

    def post(i, _):
        rs = pl.ds(pl.multiple_of(i * GLA_TILE, GLA_TILE), GLA_TILE)
        o = oacc_s[rs, :]
        ms = sum(_dot(term, bd_ref[...]) for term in _split3(o * o)[:2])
        og = og_ref[rs, :].astype(F32)
        y = o * lax.rsqrt(ms + RMS_EPS) * ng_ref[...] * (og * jax.nn.sigmoid(og))
        o_ref[rs, :] = y.astype(o_ref.dtype)
        return 0

    lax.fori_loop(0, seq // GLA_TILE, post, 0)


def _gla(gq, gk, gv, og, small, wg, bg, ng, ltri, lsum, bd, batch, seq):
    per_b = lambda b: (b, 0)
    const = lambda b: (0, 0)
    return pl.pallas_call(
        _gla_kernel,
        grid=(batch,),
        in_specs=[pl.BlockSpec((seq, GLA_QK_PAD), per_b), pl.BlockSpec((seq, GLA_QK_PAD), per_b),
                  pl.BlockSpec((seq, GLA_WIDTH), per_b), pl.BlockSpec((seq, GLA_WIDTH), per_b),
                  pl.BlockSpec((seq, LANE), per_b),
                  pl.BlockSpec(wg.shape, const), pl.BlockSpec(bg.shape, const), pl.BlockSpec(ng.shape, const),
                  pl.BlockSpec(ltri.shape, const), pl.BlockSpec(lsum.shape, const), pl.BlockSpec(bd.shape, const)],
        out_specs=pl.BlockSpec((seq, GLA_WIDTH), per_b),
        out_shape=jax.ShapeDtypeStruct((batch * seq, GLA_WIDTH), BF16),
        scratch_shapes=[pltpu.VMEM((seq, GLA_QK_PAD), BF16), pltpu.VMEM((seq, GLA_QK_PAD), BF16),
                        pltpu.VMEM((seq, GLA_QK_PAD), BF16), pltpu.VMEM((seq, GLA_QK_PAD), F32),
                        pltpu.VMEM((seq, GLA_WIDTH), F32), pltpu.VMEM((GLA_WIDTH, GLA_QK_PAD), F32)],
        compiler_params=pltpu.CompilerParams(dimension_semantics=("parallel",), vmem_limit_bytes=VMEM_LIMIT),
        name="gla",
    )(gq, gk, gv, og, small, wg, bg, ng, ltri, lsum, bd)


POOL_TILE = 256


def _pool_kernel(u_ref, band_cur_ref, band_prev_ref, w_ref, scale_ref, o_ref):
    seq = u_ref.shape[0]
    lane = lax.broadcasted_iota(jnp.int32, (1, POOL_WIDTH), 1)
    grp = lane // POOL_GROUP
    win = jnp.where(grp == 0, POOL_WINDOWS[0],
                    jnp.where(grp == 1, POOL_WINDOWS[1], jnp.where(grp == 2, POOL_WINDOWS[2], POOL_WINDOWS[3])))
    for i in range(seq // POOL_TILE):
        rs = slice(i * POOL_TILE, (i + 1) * POOL_TILE)
        u = u_ref[rs, :]
        tot = jnp.zeros((POOL_TILE, POOL_WIDTH), F32)
        for gi in range(len(POOL_WINDOWS)):
            s = _dot(band_cur_ref[gi], u)
            if i > 0:
                s = s + _dot(band_prev_ref[gi], u_ref[(i - 1) * POOL_TILE:i * POOL_TILE, :])
            tot = jnp.where(grp == gi, s, tot)
        t = i * POOL_TILE + lax.broadcasted_iota(jnp.int32, (POOL_TILE, 1), 0)
        cnt = jnp.minimum(t + 1, win).astype(F32)
        pooled = tot / cnt - u.astype(F32)
        o_ref[rs, :] = (_dot(pooled.astype(BF16), w_ref[...]) * scale_ref[...]).astype(o_ref.dtype)


def _pool(u, band_cur, band_prev, wbd, scale, batch, seq):
    per_b = lambda b: (b, 0)
    return pl.pallas_call(
        _pool_kernel,
        grid=(batch,),
        in_specs=[pl.BlockSpec((seq, POOL_WIDTH), per_b),
                  pl.BlockSpec(band_cur.shape, lambda b: (0, 0, 0)),
                  pl.BlockSpec(band_prev.shape, lambda b: (0, 0, 0)),
                  pl.BlockSpec(wbd.shape, lambda b: (0, 0)),
                  pl.BlockSpec(scale.shape, lambda b: (0, 0))],
        out_specs=pl.BlockSpec((seq, POOL_WIDTH), per_b),
        out_shape=jax.ShapeDtypeStruct((batch * seq, POOL_WIDTH), BF16),
        compiler_params=pltpu.CompilerParams(dimension_semantics=("parallel",), vmem_limit_bytes=VMEM_LIMIT),
        name="pool",
    )(u, band_cur, band_prev, wbd, scale)


def _oproj_kernel(nsa_ref, gla_ref, pool_ref, h_ref, w_ref, g_ref, b_ref, o_ref):
    acc = _dot(nsa_ref[...], w_ref[0:NSA_WIDTH, :])
    acc = acc + _dot(gla_ref[...], w_ref[NSA_WIDTH:NSA_WIDTH + GLA_WIDTH, :])
    acc = acc + _dot(pool_ref[...], w_ref[NSA_WIDTH + GLA_WIDTH:, :])
    o_ref[...] = _layer_norm(ALPHA * h_ref[...] + acc, g_ref[...], b_ref[...])


def _oproj(nsa, gla, pool, h, w, g, b, tm):
    m = h.shape[0]
    row = lambda i: (i, 0)
    const = lambda i: (0, 0)
    return pl.pallas_call(
        _oproj_kernel,
        grid=(m // tm,),
        in_specs=[pl.BlockSpec((tm, NSA_WIDTH), row), pl.BlockSpec((tm, GLA_WIDTH), row),
                  pl.BlockSpec((tm, POOL_WIDTH), row), pl.BlockSpec((tm, D_MODEL), row),
                  pl.BlockSpec((D_MODEL, D_MODEL), const, pipeline_mode=pl.Buffered(1)),
                  pl.BlockSpec((1, D_MODEL), const), pl.BlockSpec((1, D_MODEL), const)],
        out_specs=pl.BlockSpec((tm, D_MODEL), row),
        out_shape=jax.ShapeDtypeStruct((m, D_MODEL), F32),
        compiler_params=pltpu.CompilerParams(dimension_semantics=("parallel",), vmem_limit_bytes=VMEM_LIMIT),
        name="out_proj_ln",
    )(nsa, gla, pool, h, w, g, b)


FFN_CHUNK = 1024


def _ffn_kernel(h_ref, wu_ref, wd_ref, g_ref, b_ref, o_ref):
    h = h_ref[...]
    hb = h.astype(BF16)
    acc = ALPHA * h
    for c in range(D_FF // FFN_CHUNK):
        cs = slice(c * FFN_CHUNK, (c + 1) * FFN_CHUNK)
        u = jnp.maximum(_dot(hb, wu_ref[:, cs]), 0.0)
        acc = acc + _dot((u * u).astype(BF16), wd_ref[cs, :])
    o_ref[...] = _layer_norm(acc, g_ref[...], b_ref[...])


def _ffn(h, wu, wd, g, b, tm):
    m = h.shape[0]
    row = lambda i: (i, 0)
    const = lambda i: (0, 0)
    return pl.pallas_call(
        _ffn_kernel,
        grid=(m // tm,),
        in_specs=[pl.BlockSpec((tm, D_MODEL), row),
                  pl.BlockSpec((D_MODEL, D_FF), const, pipeline_mode=pl.Buffered(1)),
                  pl.BlockSpec((D_FF, D_MODEL), const, pipeline_mode=pl.Buffered(1)),
                  pl.BlockSpec((1, D_MODEL), const), pl.BlockSpec((1, D_MODEL), const)],
        out_specs=pl.BlockSpec((tm, D_MODEL), row),
        out_shape=jax.ShapeDtypeStruct((m, D_MODEL), F32),
        compiler_params=pltpu.CompilerParams(dimension_semantics=("parallel",), vmem_limit_bytes=VMEM_LIMIT),
        name="ffn_ln",
    )(h, wu, wd, g, b)


def _rope_tables(seq, scale):
    half = ROPE_DIM // 2
    inv = ROPE_THETA ** (-jnp.arange(half, dtype=F32) * 2.0 / ROPE_DIM)
    ang = jnp.arange(seq).astype(F32)[:, None] * inv[None, :]
    cos, sin = jnp.cos(ang), jnp.sin(ang)
    ones = jnp.ones((seq, HEAD_DIM - ROPE_DIM), F32)
    zeros = jnp.zeros((seq, HEAD_DIM - ROPE_DIM), F32)
    zh = jnp.zeros((seq, half), F32)
    a = jnp.concatenate([cos, cos, ones], axis=1)
    bm = jnp.concatenate([-sin, zh, zeros], axis=1)
    cm = jnp.concatenate([zh, sin, zeros], axis=1)
    reps = LANE // HEAD_DIM
    return tuple(jnp.tile(t, (1, reps)) * scale for t in (a, bm, cm))


def _static_tables(seq):
    n_sel = seq // SEL_BLOCK
    nblk = seq // CMP_STRIDE
    n_cmp = (seq - CMP_BLOCK) // CMP_STRIDE + 1
    onehot = np.zeros((seq, LANE), np.float32)
    onehot[np.arange(seq), np.arange(seq) // SEL_BLOCK] = 1.0
    c_lo = CMP_STRIDE * np.arange(n_cmp)[:, None]
    s_lo = SEL_BLOCK * np.arange(n_sel)[None, :]
    overlap = np.clip(np.minimum(c_lo + CMP_BLOCK, s_lo + SEL_BLOCK) - np.maximum(c_lo, s_lo), 0, None)
    c2s_t = np.zeros((n_sel, nblk), np.float32)
    c2s_t[:, :n_cmp] = (overlap.astype(np.float32) / CMP_BLOCK).T
    eye = np.eye(NSA_TQ, dtype=np.float32)
    perm = np.zeros((NSA_G, LANE, LANE), np.float32)
    for g in range(NSA_G):
        perm[g, g * HEAD_DIM + np.arange(HEAD_DIM), np.arange(HEAD_DIM)] = 1.0
    r = np.arange(GLA_TILE)
    same = (r[:, None] // GLA_CHUNK) == (r[None, :] // GLA_CHUNK)
    ltri = (same & (r[None, :] <= r[:, None])).astype(np.float32)
    lsum = same.astype(np.float32)
    hv = np.arange(GLA_WIDTH) // GLA_DV
    bd = (hv[:, None] == hv[None, :]).astype(np.float32) / GLA_DV
    band_cur = np.zeros((len(POOL_WINDOWS), POOL_TILE, POOL_TILE), np.float32)
    band_prev = np.zeros_like(band_cur)
    pi = np.arange(POOL_TILE)
    for gi, w in enumerate(POOL_WINDOWS):
        d = pi[:, None] - pi[None, :]
        band_cur[gi] = ((d >= 0) & (d < w)).astype(np.float32)
        band_prev[gi] = ((d + POOL_TILE >= 0) & (d + POOL_TILE < w)).astype(np.float32)
    as_bf = lambda a: jnp.asarray(a, BF16)
    return dict(onehot=as_bf(onehot), c2s_t=as_bf(c2s_t), eye=as_bf(eye), perm=as_bf(perm), ltri=as_bf(ltri), lsum=as_bf(lsum),
                bd=as_bf(bd), band_cur=as_bf(band_cur), band_prev=as_bf(band_prev))


def _block_diag2(w):
    z = jnp.zeros_like(w)
    return jnp.concatenate([jnp.concatenate([w, z], axis=-1), jnp.concatenate([z, w], axis=-1)], axis=-2)


def _layer_weights(w_in, cmp_pos, cmp_w1, cmp_w2, w_gate2, b_gate, norm_g, pool_w, pool_scale):
    o_nq, o_nkv, o_gate = 0, NSA_WIDTH, NSA_WIDTH + 768
    o_gq = o_gate + 18
    o_gk = o_gq + 192
    o_gv = o_gk + 192
    o_glr = o_gv + GLA_WIDTH
    o_og = o_glr + GLA_RANK
    o_pu = o_og + GLA_WIDTH
    zc = lambda n: jnp.zeros((D_MODEL, n), F32)
    small = jnp.concatenate([w_in[:, o_gate:o_gq], zc(SMALL_GLR0 - 18), w_in[:, o_glr:o_og],
                             zc(LANE - SMALL_GLR0 - GLA_RANK)], axis=1)
    wp = jnp.concatenate([w_in[:, o_nq:o_gate], small,
                          w_in[:, o_gq:o_gk], zc(GLA_QK_PAD - 192), w_in[:, o_gk:o_gv], zc(GLA_QK_PAD - 192),
                          w_in[:, o_gv:o_glr], w_in[:, o_og:o_pu], w_in[:, o_pu:]], axis=1).astype(BF16)
    pos2 = jnp.concatenate([cmp_pos, cmp_pos], axis=-1)
    w1bd = _block_diag2(cmp_w1.reshape(2, CMP_BLOCK, HEAD_DIM, cmp_w1.shape[-1])).astype(BF16)
    w2bd = _block_diag2(cmp_w2).astype(BF16)
    wg = jnp.zeros((LANE, GLA_QK_PAD), F32).at[SMALL_GLR0:SMALL_GLR0 + GLA_RANK, :192].set(w_gate2)
    bg = jnp.zeros((1, GLA_QK_PAD), F32).at[0, :192].set(b_gate)
    ng = norm_g.reshape(1, GLA_WIDTH)
    pw = jnp.zeros((POOL_WIDTH, POOL_WIDTH), F32)
    for gi in range(len(POOL_WINDOWS)):
        sl = slice(gi * POOL_GROUP, (gi + 1) * POOL_GROUP)
        pw = pw.at[sl, sl].set(pool_w[gi])
    return dict(wp=wp, pos2=pos2, w1bd=w1bd, w2bd=w2bd, wg=wg, bg=bg, ng=ng, pw=pw.astype(BF16),
                ps=pool_scale.reshape(1, POOL_WIDTH))


@jax.jit
def kernel(x, w_in, cmp_pos, cmp_w1, cmp_w2, gla_w_gate2, gla_b_gate, gla_norm_g, pool_w, pool_scale,
           w_out, ln1_g, ln1_b, w_up, w_down, ln2_g, ln2_b):
    batch, seq, d = x.shape
    assert d == D_MODEL and seq % GLA_TILE == 0 and seq >= WIN_SPAN and seq // SEL_BLOCK <= LANE
    tm = 512
    st = _static_tables(seq)
    rope_q = _rope_tables(seq, HEAD_DIM ** -0.5 * LOG2E)
    rope_k = _rope_tables(seq, 1.0)
    h = x.reshape(batch * seq, d)
    for l in range(w_in.shape[0]):
        lw = _layer_weights(w_in[l], cmp_pos[l], cmp_w1[l], cmp_w2[l], gla_w_gate2[l], gla_b_gate[l],
                            gla_norm_g[l], pool_w[l], pool_scale[l])
        q, cmp_k, cmp_v, sel, win, small, gq, gk, gv, og, pu = _proj(h, lw["wp"], rope_q, rope_k, seq, tm)
        kc = _compress(cmp_k, cmp_v, lw["pos2"], lw["w1bd"], lw["w2bd"], batch, seq)
        nsa = _nsa(q, kc, sel, win, small, st["onehot"], st["c2s_t"], st["eye"], st["perm"], batch, seq)
        gla = _gla(gq, gk, gv, og, small, lw["wg"], lw["bg"], lw["ng"], st["ltri"], st["lsum"], st["bd"],
                   batch, seq)
        pool = _pool(pu, st["band_cur"], st["band_prev"], lw["pw"], lw["ps"], batch, seq)
        h = _oproj(nsa, gla, pool, h, w_out[l].astype(BF16), ln1_g[l].reshape(1, d), ln1_b[l].reshape(1, d), tm)
        h = _ffn(h, w_up[l].astype(BF16), w_down[l].astype(BF16), ln2_g[l].reshape(1, d), ln2_b[l].reshape(1, d), tm)
    return h.reshape(batch, seq, d)
```

```python
import functools
import numpy as np
import jax
import jax.numpy as jnp
from jax import lax
from jax.experimental import pallas as pl
from jax.experimental.pallas import tpu as pltpu

F32 = jnp.float32
BF16 = jnp.bfloat16

D_MODEL = 1024
HEAD_DIM = 64
NSA_G = 2
NSA_H = 3
NSA_WIDTH = NSA_G * NSA_H * HEAD_DIM
CMP_BLOCK = 32
CMP_STRIDE = 16
SEL_BLOCK = 64
SEL_TOP_N = 16
WINDOW = 512
FORCE_SCORE = 1.0e4
GLA_HEADS = 6
GLA_DK = 32
GLA_DV = 64
GLA_WIDTH = GLA_HEADS * GLA_DV
GLA_RANK = 16
GLA_GATE_NORM = 16.0
GLA_CHUNK = 64
GLA_QK_PAD = 256
POOL_WINDOWS = (2, 4, 8, 16)
POOL_GROUP = 64
POOL_WIDTH = 256
D_FF = 4 * D_MODEL
ROPE_THETA = 500000.0
ROPE_DIM = HEAD_DIM // 4
LN_EPS = 1e-5
RMS_EPS = 1e-6
DEPTH = 4
ALPHA = (2 * DEPTH) ** 0.25

LANE = 128
NEG = -1e30
SEL_NEG = -30000.0

VMEM_LIMIT = 48 * 1024 * 1024

SMALL_GATE0 = 0
SMALL_GLR0 = 32
PROJ_COLS = 22 * LANE


def _nt(a, b):
    return lax.dot_general(a, b, (((1,), (1,)), ((), ())), preferred_element_type=F32)


def _tn(a, b):
    return lax.dot_general(a, b, (((0,), (0,)), ((), ())), preferred_element_type=F32)


def _dot(a, b):
    return jnp.dot(a, b, preferred_element_type=F32)


def _split3(x):
    hi = x.astype(BF16)
    r = x - hi.astype(F32)
    mid = r.astype(BF16)
    lo = (r - mid.astype(F32)).astype(BF16)
    return hi, mid, lo


def _layer_norm(y, g, b):
    mu = jnp.mean(y, axis=-1, keepdims=True)
    d = y - mu
    var = jnp.mean(d * d, axis=-1, keepdims=True)
    return d * lax.rsqrt(var + LN_EPS) * g + b


def _rope(x, a, bm, cm):
    return x * a + pltpu.roll(x, LANE - ROPE_DIM // 2, 1) * bm + pltpu.roll(x, ROPE_DIM // 2, 1) * cm


def _proj_kernel(h_ref, w_ref, qa_ref, qb_ref, qc_ref, ka_ref, kb_ref, kc_ref,
                 q_out, cmpk_out, cmpv_out, sel_out, win_out, small_out, gq_out, gk_out, gv_out, og_out, pu_out):
    hb = h_ref[...].astype(BF16)
    plan = ([(q_out, j, "q") for j in range(3)]
            + [(cmpk_out, 0, "k"), (cmpv_out, 0, None), (sel_out, 0, "k"), (sel_out, 1, None),
               (win_out, 0, "k"), (win_out, 1, None), (small_out, 0, None)]
            + [(gq_out, j, None) for j in range(2)] + [(gk_out, j, None) for j in range(2)]
            + [(gv_out, j, None) for j in range(3)] + [(og_out, j, None) for j in range(3)]
            + [(pu_out, j, None) for j in range(2)])
    for c in range(0, len(plan), 2):
        acc = _dot(hb, w_ref[:, c * LANE:(c + 2) * LANE])
        for half in range(2):
            ref, j, kind = plan[c + half]
            x = acc[:, half * LANE:(half + 1) * LANE]
            if kind == "q":
                x = _rope(x, qa_ref[...], qb_ref[...], qc_ref[...])
            elif kind == "k":
                x = _rope(x, ka_ref[...], kb_ref[...], kc_ref[...])
            ref[:, j * LANE:(j + 1) * LANE] = x.astype(ref.dtype)


def _proj(h, w, rope_q, rope_k, seq, tm):
    m = h.shape[0]
    tiles_per_seq = seq // tm
    row = lambda i: (i, 0)
    tab = lambda i: (i % tiles_per_seq, 0)
    widths = [(NSA_WIDTH, BF16), (LANE, F32), (LANE, F32), (256, BF16), (256, BF16), (LANE, F32),
              (GLA_QK_PAD, BF16), (GLA_QK_PAD, BF16), (GLA_WIDTH, BF16), (GLA_WIDTH, BF16), (POOL_WIDTH, BF16)]
    return pl.pallas_call(
        _proj_kernel,
        grid=(m // tm,),
        in_specs=[pl.BlockSpec((tm, D_MODEL), row),
                  pl.BlockSpec((D_MODEL, PROJ_COLS), lambda i: (0, 0), pipeline_mode=pl.Buffered(1))]
                 + [pl.BlockSpec((tm, LANE), tab)] * 6,
        out_specs=[pl.BlockSpec((tm, wd), row) for wd, _ in widths],
        out_shape=[jax.ShapeDtypeStruct((m, wd), dt) for wd, dt in widths],
        compiler_params=pltpu.CompilerParams(dimension_semantics=("parallel",), vmem_limit_bytes=VMEM_LIMIT),
        name="in_proj",
    )(h, w, *rope_q, *rope_k)


def _cmp_kernel(k_ref, v_ref, pos_ref, w1_ref, w2_ref, o_ref):
    nblk = o_ref.shape[0]
    half = CMP_BLOCK // 2
    for kvi, x_ref in enumerate((k_ref, v_ref)):
        lanes = slice(kvi * LANE, (kvi + 1) * LANE)
        ha = jnp.zeros((nblk, LANE), F32)
        hb = jnp.zeros((nblk, LANE), F32)
        for r in range(half):
            t = x_ref[pl.ds(r, nblk, stride=CMP_STRIDE), :]
            ha = ha + _dot((t + pos_ref[kvi, r:r + 1, :]).astype(BF16), w1_ref[kvi, r])
            hb = hb + _dot((t + pos_ref[kvi, half + r:half + r + 1, :]).astype(BF16), w1_ref[kvi, half + r])
        hid = ha + pltpu.roll(hb, nblk - 1, 0)
        hid = jax.nn.gelu(hid, approximate=True)
        o_ref[:, lanes] = _dot(hid.astype(BF16), w2_ref[kvi])


def _compress(cmp_k, cmp_v, pos2, w1bd, w2bd, batch, seq):
    nblk = seq // CMP_STRIDE
    return pl.pallas_call(
        _cmp_kernel,
        grid=(batch,),
        in_specs=[pl.BlockSpec((seq, LANE), lambda b: (b, 0)), pl.BlockSpec((seq, LANE), lambda b: (b, 0)),
                  pl.BlockSpec((2, CMP_BLOCK, LANE), lambda b: (0, 0, 0)),
                  pl.BlockSpec((2, CMP_BLOCK, LANE, LANE), lambda b: (0, 0, 0, 0)),
                  pl.BlockSpec((2, LANE, LANE), lambda b: (0, 0, 0))],
        out_specs=pl.BlockSpec((nblk, 256), lambda b: (b, 0)),
        out_shape=jax.ShapeDtypeStruct((batch * nblk, 256), F32),
        compiler_params=pltpu.CompilerParams(dimension_semantics=("parallel",), vmem_limit_bytes=VMEM_LIMIT),
        name="nsa_compress",
    )(cmp_k, cmp_v, pos2, w1bd, w2bd)


NSA_TQ = 256
NSA_TK = 512
WIN_SPAN = WINDOW + NSA_TQ
LOG2E = 1.4426950408889634


def _nsa_kernel(q_ref, kc_ref, sel_ref, win_ref, oh_ref, g_ref, c2s_ref, eye_ref, perm_ref, o_ref,
                kaug, vsel, vwin, vcmp, qaug, gate_scr, mix_scr):
    tq, tk = NSA_TQ, NSA_TK
    rows3 = NSA_H * tq
    qi = pl.program_id(1)
    n_sel = oh_ref.shape[0] // SEL_BLOCK
    lane2 = lax.broadcasted_iota(jnp.int32, (1, 2 * LANE), 1)
    ones_half = jnp.where(lane2 >= LANE, 1.0, 0.0)

    @pl.when(qi == 0)
    def _():
        kaug[:, 0:LANE] = sel_ref[:, 0:LANE]
        kaug[:, LANE:2 * LANE] = oh_ref[...]
        for g in range(NSA_G):
            vsel[g] = (_dot(sel_ref[:, LANE:2 * LANE], perm_ref[g]) + ones_half).astype(BF16)
            vwin[g] = (_dot(win_ref[:, LANE:2 * LANE], perm_ref[g]) + ones_half).astype(BF16)
            vcmp[g] = _dot(kc_ref[:, LANE:2 * LANE].astype(BF16), perm_ref[g, :, 0:LANE]).astype(BF16)

    q0 = qi * tq
    t3 = q0 + (lax.broadcasted_iota(jnp.int32, (rows3, 1), 0) & (tq - 1))
    t_lane = q0 + lax.broadcasted_iota(jnp.int32, (1, tq), 1)
    gates = jax.nn.sigmoid(g_ref[...])
    for g in range(NSA_G):
        for h in range(NSA_H):
            for br in range(3):
                col = (g * NSA_H + h) * 3 + br
                gate_scr[g, br, h * tq:(h + 1) * tq, :] = jnp.broadcast_to(gates[:, col:col + 1], (tq, LANE))
    kc_b = kc_ref[:, 0:LANE].astype(BF16)
    for g in range(NSA_G):
        glanes = slice(g * HEAD_DIM, (g + 1) * HEAD_DIM)
        qaug[g, :, 0:LANE] = jnp.zeros((rows3, LANE), BF16)
        for h in range(NSA_H):
            c = (NSA_H * g + h) * HEAD_DIM
            qaug[g, h * tq:(h + 1) * tq, glanes] = q_ref[:, c:c + HEAD_DIM]

    ks = pl.multiple_of(jnp.maximum(q0 - WINDOW, 0), tq)
    kw = win_ref[pl.ds(ks, WIN_SPAN), 0:LANE]
    t1 = q0 + lax.broadcasted_iota(jnp.int32, (tq, 1), 0)
    diff = t1 - (ks + lax.broadcasted_iota(jnp.int32, (1, WIN_SPAN), 1))
    wbias = jnp.where((diff >= 0) & (diff < WINDOW), 0.0, NEG)
    wbias = jnp.concatenate([wbias] * NSA_H, axis=0)
    o_win = []
    for g in range(NSA_G):
        sw = _nt(qaug[g, :, 0:LANE], kw) + wbias
        pw = jnp.exp2(sw - jnp.max(sw, axis=1, keepdims=True)).astype(BF16)
        a_win = _dot(pw, vwin[g, pl.ds(ks, WIN_SPAN), :])
        o_win.append(a_win[:, 0:LANE] * (gate_scr[g, 2] / a_win[:, LANE:2 * LANE]))

    for g in range(NSA_G):
        s = _nt(qaug[g, :, 0:LANE], kc_b)
        ncol = lax.broadcasted_iota(jnp.int32, (1, s.shape[1]), 1)
        cmask = (CMP_STRIDE * ncol + (CMP_BLOCK - 1)) <= t3
        sm = jnp.where(cmask, s, NEG)
        m = jnp.max(sm, axis=1, keepdims=True)
        e = jnp.where(cmask, jnp.exp2(sm - m), 0.0)
        p = e / jnp.maximum(jnp.sum(e, axis=1, keepdims=True), 1e-30)
        o_cmp = _dot(p.astype(BF16), vcmp[g])
        mix_scr[g] = gate_scr[g, 0] * o_cmp + o_win[g]

        psum = p[0:tq] + p[tq:2 * tq] + p[2 * tq:3 * tq]
        c2s = c2s_ref[...]
        imp = sum(_nt(c2s, term) for term in _split3(psum))
        jb = lax.broadcasted_iota(jnp.int32, (n_sel, 1), 0)
        cur = t_lane // SEL_BLOCK
        imp = jnp.where((jb == 0) | (jb == cur) | (jb == cur - 1), FORCE_SCORE, imp)
        imp = jnp.where(jb > cur, -FORCE_SCORE, imp)
        rank = jnp.zeros(imp.shape, F32)
        for i in range(n_sel):
            ri = imp[i:i + 1, :]
            ahead = (ri > imp) | ((ri == imp) & (jb > i))
            rank = rank + jnp.where(ahead, 1.0, 0.0)
        sel_t = jnp.where(rank < float(min(SEL_TOP_N, n_sel)), 1.0, 0.0).astype(BF16)
        sel_t = jnp.concatenate([sel_t, jnp.zeros((LANE - n_sel, tq), BF16)], axis=0)
        sel = _nt(eye_ref[...], sel_t)
        bias = ((sel - 1.0) * (-SEL_NEG)).astype(BF16)
        for h in range(NSA_H):
            qaug[g, h * tq:(h + 1) * tq, LANE:2 * LANE] = bias

    last = (q0 + tq - 1) // tk

    def selected(n_chunks):
        lo = (n_chunks - 1) * tk
        kpos = lo + lax.broadcasted_iota(jnp.int32, (1, tk), 1)
        cbias = jnp.concatenate([jnp.where(kpos <= t1, 0.0, NEG)] * NSA_H, axis=0)
        heads = []
        for g in range(NSA_G):
            qa = qaug[g]
            sb = _nt(qa, kaug[lo:lo + tk, :]) + cbias
            blocks = [sb[:, j * LANE:(j + 1) * LANE] for j in range(tk // LANE)]
            if lo:
                sa = _nt(qa, kaug[0:lo, :])
                blocks += [sa[:, j * LANE:(j + 1) * LANE] for j in range(lo // LANE)]
            m = jnp.max(functools.reduce(jnp.maximum, blocks), axis=1, keepdims=True)
            acc = _dot(jnp.exp2(sb - m).astype(BF16), vsel[g, lo:lo + tk, :])
            if lo:
                acc = acc + _dot(jnp.exp2(sa - m).astype(BF16), vsel[g, 0:lo, :])
            mix = mix_scr[g] + acc[:, 0:LANE] * (gate_scr[g, 1] / acc[:, LANE:2 * LANE])
            heads += [mix[h * tq:(h + 1) * tq] for h in range(NSA_H)]
        low = lax.broadcasted_iota(jnp.int32, (1, LANE), 1) < HEAD_DIM
        for j in range(len(heads) // 2):
            pair = jnp.where(low, heads[2 * j], heads[2 * j + 1])
            o_ref[:, j * LANE:(j + 1) * LANE] = pair.astype(o_ref.dtype)

    for n_chunks in range(1, kaug.shape[0] // tk + 1):
        pl.when(last == n_chunks - 1)(functools.partial(selected, n_chunks))


def _nsa(q, kc, sel, win, small, onehot, c2s_t, eye, perm, batch, seq):
    nq = seq // NSA_TQ
    nblk = seq // CMP_STRIDE
    rows3 = NSA_H * NSA_TQ
    qrow = lambda b, i: (b * nq + i, 0)
    per_b = lambda b, i: (b, 0)
    const = lambda b, i: (0, 0)
    return pl.pallas_call(
        _nsa_kernel,
        grid=(batch, nq),
        in_specs=[pl.BlockSpec((NSA_TQ, NSA_WIDTH), qrow),
                  pl.BlockSpec((nblk, 256), per_b),
                  pl.BlockSpec((seq, 256), per_b),
                  pl.BlockSpec((seq, 256), per_b),
                  pl.BlockSpec((seq, LANE), const),
                  pl.BlockSpec((NSA_TQ, LANE), qrow),
                  pl.BlockSpec(c2s_t.shape, const),
                  pl.BlockSpec((NSA_TQ, NSA_TQ), const),
                  pl.BlockSpec(perm.shape, lambda b, i: (0, 0, 0))],
        out_specs=pl.BlockSpec((NSA_TQ, NSA_WIDTH), qrow),
        out_shape=jax.ShapeDtypeStruct((batch * seq, NSA_WIDTH), BF16),
        scratch_shapes=[pltpu.VMEM((seq, 2 * LANE), BF16),
                        pltpu.VMEM((NSA_G, seq, 2 * LANE), BF16),
                        pltpu.VMEM((NSA_G, seq, 2 * LANE), BF16),
                        pltpu.VMEM((NSA_G, nblk, LANE), BF16),
                        pltpu.VMEM((NSA_G, rows3, 2 * LANE), BF16),
                        pltpu.VMEM((NSA_G, 3, rows3, LANE), F32),
                        pltpu.VMEM((NSA_G, rows3, LANE), F32)],
        compiler_params=pltpu.CompilerParams(dimension_semantics=("parallel", "arbitrary"),
                                             vmem_limit_bytes=VMEM_LIMIT),
        name="nsa_attention",
    )(q, kc, sel, win, onehot, small, c2s_t, eye, perm)


GLA_TILE = 256


def _log_sigmoid(x):
    return jnp.minimum(x, 0.0) - jnp.log(1.0 + jnp.exp(-jnp.abs(x)))


def _gla_kernel(q_ref, k_ref, v_ref, og_ref, small_ref, wg_ref, bg_ref, ng_ref, ltri_ref, lsum_ref, bd_ref,
                o_ref, qd_s, ke_s, dec_s, oacc_s, st_s):
    seq = q_ref.shape[0]
    ch = GLA_CHUNK
    nh = GLA_HEADS
    tile = GLA_TILE
    qk_head = lax.broadcasted_iota(jnp.int32, (1, GLA_QK_PAD), 1) // GLA_DK
    v_head = lax.broadcasted_iota(jnp.int32, (1, GLA_WIDTH), 1) // GLA_DV
    ri = lax.broadcasted_iota(jnp.int32, (tile, 1), 0)
    cj = lax.broadcasted_iota(jnp.int32, (1, tile), 1)
    intra = (cj <= ri) & ((cj // ch) == (ri // ch))
    intra = jnp.concatenate([intra] * nh, axis=1)

    def pre(i, _):
        rs = pl.ds(pl.multiple_of(i * tile, tile), tile)
        x_hi, x_mid, _ = _split3(small_ref[rs, :])
        w_hi, w_mid, _ = _split3(wg_ref[...])
        z = _dot(x_hi, w_hi) + (_dot(x_hi, w_mid) + _dot(x_mid, w_hi)) + bg_ref[...]
        gl = _log_sigmoid(z) * (1.0 / GLA_GATE_NORM)
        g_hi, g_mid, _ = _split3(gl)
        bc = _dot(ltri_ref[...], g_hi) + _dot(ltri_ref[...], g_mid)
        bl = _dot(lsum_ref[...], g_hi) + _dot(lsum_ref[...], g_mid)
        dec = jnp.exp(bl)
        qd = (q_ref[rs, :].astype(F32) * (GLA_DK ** -0.5) * jnp.exp(bc)).astype(BF16)
        k_inv = k_ref[rs, :].astype(F32) * jnp.exp(-bc)
        ki = k_inv.astype(BF16)
        qd_s[rs, :] = qd
        ke_s[rs, :] = (k_inv * dec).astype(BF16)
        dec_s[rs, :] = dec
        v = v_ref[rs, :]
        k_heads = jnp.concatenate([jnp.where(qk_head == h, ki, jnp.zeros_like(ki)) for h in range(nh)], axis=0)
        v_heads = jnp.concatenate([jnp.where(v_head == h, v, jnp.zeros_like(v)) for h in range(nh)], axis=0)
        a = jnp.where(intra, _nt(qd, k_heads), 0.0)
        oacc_s[rs, :] = _dot(a.astype(BF16), v_heads)
        return 0

    lax.fori_loop(0, seq // tile, pre, 0)

    st_mask = (lax.broadcasted_iota(jnp.int32, (GLA_WIDTH, 1), 0) // GLA_DV) == qk_head
    st_s[...] = jnp.zeros(st_s.shape, F32)
    per_iter = tile // ch

    def scan(i, _):
        st = st_s[...]
        for c in range(per_iter):
            r0 = pl.multiple_of(i * tile, tile) + c * ch
            rs = pl.ds(r0, ch)
            oacc_s[rs, :] = oacc_s[rs, :] + _nt(qd_s[rs, :], st.astype(BF16))
            dt = _tn(v_ref[rs, :], ke_s[rs, :])
            st = st * dec_s[pl.ds(r0, 1), :] + jnp.where(st_mask, dt, 0.0)
        st_s[...] = st
        return 0

    lax.fori_loop(0, seq // tile, scan, 0)

    def post(i, _):
        rs = pl.ds(pl.multiple_of(i * tile, tile), tile)
        o = oacc_s[rs, :]
        ms = _dot((o * o).astype(BF16), bd_ref[...])
        og = og_ref[rs, :].astype(F32)
        y = o * lax.rsqrt(ms + RMS_EPS) * ng_ref[...] * (og * jax.nn.sigmoid(og))
        o_ref[rs, :] = y.astype(o_ref.dtype)
        return 0

    lax.fori_loop(0, seq // tile, post, 0)


def _gla(gq, gk, gv, og, small, wg, bg, ng, ltri, lsum, bd, batch, seq):
    per_b = lambda b: (b, 0)
    const = lambda b: (0, 0)
    return pl.pallas_call(
        _gla_kernel,
        grid=(batch,),
        in_specs=[pl.BlockSpec((seq, GLA_QK_PAD), per_b), pl.BlockSpec((seq, GLA_QK_PAD), per_b),
                  pl.BlockSpec((seq, GLA_WIDTH), per_b), pl.BlockSpec((seq, GLA_WIDTH), per_b),
                  pl.BlockSpec((seq, LANE), per_b),
                  pl.BlockSpec(wg.shape, const), pl.BlockSpec(bg.shape, const), pl.BlockSpec(ng.shape, const),
                  pl.BlockSpec(ltri.shape, const), pl.BlockSpec(lsum.shape, const), pl.BlockSpec(bd.shape, const)],
        out_specs=pl.BlockSpec((seq, GLA_WIDTH), per_b),
        out_shape=jax.ShapeDtypeStruct((batch * seq, GLA_WIDTH), BF16),
        scratch_shapes=[pltpu.VMEM((seq, GLA_QK_PAD), BF16), pltpu.VMEM((seq, GLA_QK_PAD), BF16),
                        pltpu.VMEM((seq, GLA_QK_PAD), F32),
                        pltpu.VMEM((seq, GLA_WIDTH), F32), pltpu.VMEM((GLA_WIDTH, GLA_QK_PAD), F32)],
        compiler_params=pltpu.CompilerParams(dimension_semantics=("parallel",), vmem_limit_bytes=VMEM_LIMIT),
        name="gla",
    )(gq, gk, gv, og, small, wg, bg, ng, ltri, lsum, bd)


POOL_TILE = 256


def _pool_kernel(u_ref, band_cur_ref, band_prev_ref, w_ref, scale_ref, o_ref):
    seq = u_ref.shape[0]
    lane = lax.broadcasted_iota(jnp.int32, (1, POOL_WIDTH), 1)
    grp = lane // POOL_GROUP
    win = jnp.where(grp == 0, POOL_WINDOWS[0],
                    jnp.where(grp == 1, POOL_WINDOWS[1], jnp.where(grp == 2, POOL_WINDOWS[2], POOL_WINDOWS[3])))
    for i in range(seq // POOL_TILE):
        rs = slice(i * POOL_TILE, (i + 1) * POOL_TILE)
        u = u_ref[rs, :]
        tot = jnp.zeros((POOL_TILE, POOL_WIDTH), F32)
        for gi in range(len(POOL_WINDOWS)):
            s = _dot(band_cur_ref[gi], u)
            if i > 0:
                s = s + _dot(band_prev_ref[gi], u_ref[(i - 1) * POOL_TILE:i * POOL_TILE, :])
            tot = jnp.where(grp == gi, s, tot)
        t = i * POOL_TILE + lax.broadcasted_iota(jnp.int32, (POOL_TILE, 1), 0)
        cnt = jnp.minimum(t + 1, win).astype(F32)
        pooled = tot / cnt - u.astype(F32)
        o_ref[rs, :] = (_dot(pooled.astype(BF16), w_ref[...]) * scale_ref[...]).astype(o_ref.dtype)


def _pool(u, band_cur, band_prev, wbd, scale, batch, seq):
    per_b = lambda b: (b, 0)
    return pl.pallas_call(
        _pool_kernel,
        grid=(batch,),
        in_specs=[pl.BlockSpec((seq, POOL_WIDTH), per_b),
                  pl.BlockSpec(band_cur.shape, lambda b: (0, 0, 0)),
                  pl.BlockSpec(band_prev.shape, lambda b: (0, 0, 0)),
                  pl.BlockSpec(wbd.shape, lambda b: (0, 0)),
                  pl.BlockSpec(scale.shape, lambda b: (0, 0))],
        out_specs=pl.BlockSpec((seq, POOL_WIDTH), per_b),
        out_shape=jax.ShapeDtypeStruct((batch * seq, POOL_WIDTH), BF16),
        compiler_params=pltpu.CompilerParams(dimension_semantics=("parallel",), vmem_limit_bytes=VMEM_LIMIT),
        name="pool",
    )(u, band_cur, band_prev, wbd, scale)


FFN_CHUNK = 1024


def _mix_ffn_kernel(nsa_ref, gla_ref, pool_ref, h_ref, wo_ref, g1_ref, b1_ref, wu_ref, wd_ref, g2_ref, b2_ref,
                    o_ref):
    acc = _dot(nsa_ref[...], wo_ref[0:NSA_WIDTH, :])
    acc = acc + _dot(gla_ref[...], wo_ref[NSA_WIDTH:NSA_WIDTH + GLA_WIDTH, :])
    acc = acc + _dot(pool_ref[...], wo_ref[NSA_WIDTH + GLA_WIDTH:, :])
    h = _layer_norm(ALPHA * h_ref[...] + acc, g1_ref[...], b1_ref[...])
    hb = h.astype(BF16)
    acc = ALPHA * h
    for c in range(D_FF // FFN_CHUNK):
        cs = slice(c * FFN_CHUNK, (c + 1) * FFN_CHUNK)
        u = jnp.maximum(_dot(hb, wu_ref[:, cs]), 0.0)
        acc = acc + _dot((u * u).astype(BF16), wd_ref[cs, :])
    o_ref[...] = _layer_norm(acc, g2_ref[...], b2_ref[...])


def _mix_ffn(nsa, gla, pool, h, wo, g1, b1, wu, wd, g2, b2, tm):
    m = h.shape[0]
    row = lambda i: (i, 0)
    const = lambda i: (0, 0)
    vec = pl.BlockSpec((1, D_MODEL), const)
    resident = lambda shape: pl.BlockSpec(shape, const, pipeline_mode=pl.Buffered(1))
    return pl.pallas_call(
        _mix_ffn_kernel,
        grid=(m // tm,),
        in_specs=[pl.BlockSpec((tm, NSA_WIDTH), row), pl.BlockSpec((tm, GLA_WIDTH), row),
                  pl.BlockSpec((tm, POOL_WIDTH), row), pl.BlockSpec((tm, D_MODEL), row),
                  resident((D_MODEL, D_MODEL)), vec, vec,
                  resident((D_MODEL, D_FF)), resident((D_FF, D_MODEL)), vec, vec],
        out_specs=pl.BlockSpec((tm, D_MODEL), row),
        out_shape=jax.ShapeDtypeStruct((m, D_MODEL), F32),
        compiler_params=pltpu.CompilerParams(dimension_semantics=("parallel",), vmem_limit_bytes=VMEM_LIMIT),
        name="out_proj_ffn",
    )(nsa, gla, pool, h, wo, g1, b1, wu, wd, g2, b2)


def _rope_tables(seq, scale):
    half = ROPE_DIM // 2
    inv = ROPE_THETA ** (-jnp.arange(half, dtype=F32) * 2.0 / ROPE_DIM)
    ang = jnp.arange(seq).astype(F32)[:, None] * inv[None, :]
    cos, sin = jnp.cos(ang), jnp.sin(ang)
    ones = jnp.ones((seq, HEAD_DIM - ROPE_DIM), F32)
    zeros = jnp.zeros((seq, HEAD_DIM - ROPE_DIM), F32)
    zh = jnp.zeros((seq, half), F32)
    a = jnp.concatenate([cos, cos, ones], axis=1)
    bm = jnp.concatenate([-sin, zh, zeros], axis=1)
    cm = jnp.concatenate([zh, sin, zeros], axis=1)
    reps = LANE // HEAD_DIM
    return tuple(jnp.tile(t, (1, reps)) * scale for t in (a, bm, cm))


def _static_tables(seq):
    n_sel = seq // SEL_BLOCK
    nblk = seq // CMP_STRIDE
    n_cmp = (seq - CMP_BLOCK) // CMP_STRIDE + 1
    onehot = np.zeros((seq, LANE), np.float32)
    onehot[np.arange(seq), np.arange(seq) // SEL_BLOCK] = 1.0
    c_lo = CMP_STRIDE * np.arange(n_cmp)[:, None]
    s_lo = SEL_BLOCK * np.arange(n_sel)[None, :]
    overlap = np.clip(np.minimum(c_lo + CMP_BLOCK, s_lo + SEL_BLOCK) - np.maximum(c_lo, s_lo), 0, None)
    c2s_t = np.zeros((n_sel, nblk), np.float32)
    c2s_t[:, :n_cmp] = (overlap.astype(np.float32) / CMP_BLOCK).T
    eye = np.eye(NSA_TQ, dtype=np.float32)
    perm = np.zeros((NSA_G, LANE, 2 * LANE), np.float32)
    for g in range(NSA_G):
        for rep in range(2):
            perm[g, g * HEAD_DIM + np.arange(HEAD_DIM), rep * HEAD_DIM + np.arange(HEAD_DIM)] = 1.0
    r = np.arange(GLA_TILE)
    same = (r[:, None] // GLA_CHUNK) == (r[None, :] // GLA_CHUNK)
    ltri = (same & (r[None, :] <= r[:, None])).astype(np.float32)
    lsum = same.astype(np.float32)
    hv = np.arange(GLA_WIDTH) // GLA_DV
    bd = (hv[:, None] == hv[None, :]).astype(np.float32) / GLA_DV
    band_cur = np.zeros((len(POOL_WINDOWS), POOL_TILE, POOL_TILE), np.float32)
    band_prev = np.zeros_like(band_cur)
    pi = np.arange(POOL_TILE)
    for gi, w in enumerate(POOL_WINDOWS):
        d = pi[:, None] - pi[None, :]
        band_cur[gi] = ((d >= 0) & (d < w)).astype(np.float32)
        band_prev[gi] = ((d + POOL_TILE >= 0) & (d + POOL_TILE < w)).astype(np.float32)
    as_bf = lambda a: jnp.asarray(a, BF16)
    return dict(onehot=as_bf(onehot), c2s_t=as_bf(c2s_t), eye=as_bf(eye), perm=as_bf(perm), ltri=as_bf(ltri), lsum=as_bf(lsum),
                bd=as_bf(bd), band_cur=as_bf(band_cur), band_prev=as_bf(band_prev))


def _block_diag2(w):
    z = jnp.zeros_like(w)
    return jnp.concatenate([jnp.concatenate([w, z], axis=-1), jnp.concatenate([z, w], axis=-1)], axis=-2)


def _layer_weights(w_in, cmp_pos, cmp_w1, cmp_w2, w_gate2, b_gate, norm_g, pool_w, pool_scale):
    o_nq, o_nkv, o_gate = 0, NSA_WIDTH, NSA_WIDTH + 768
    o_gq = o_gate + 18
    o_gk = o_gq + 192
    o_gv = o_gk + 192
    o_glr = o_gv + GLA_WIDTH
    o_og = o_glr + GLA_RANK
    o_pu = o_og + GLA_WIDTH
    zc = lambda n: jnp.zeros((D_MODEL, n), F32)
    small = jnp.concatenate([w_in[:, o_gate:o_gq], zc(SMALL_GLR0 - 18), w_in[:, o_glr:o_og],
                             zc(LANE - SMALL_GLR0 - GLA_RANK)], axis=1)
    wp = jnp.concatenate([w_in[:, o_nq:o_gate], small,
                          w_in[:, o_gq:o_gk], zc(GLA_QK_PAD - 192), w_in[:, o_gk:o_gv], zc(GLA_QK_PAD - 192),
                          w_in[:, o_gv:o_glr], w_in[:, o_og:o_pu], w_in[:, o_pu:]], axis=1).astype(BF16)
    pos2 = jnp.concatenate([cmp_pos, cmp_pos], axis=-1)
    w1bd = _block_diag2(cmp_w1.reshape(2, CMP_BLOCK, HEAD_DIM, cmp_w1.shape[-1])).astype(BF16)
    w2bd = _block_diag2(cmp_w2).astype(BF16)
    wg = jnp.zeros((LANE, GLA_QK_PAD), F32).at[SMALL_GLR0:SMALL_GLR0 + GLA_RANK, :192].set(w_gate2)
    bg = jnp.zeros((1, GLA_QK_PAD), F32).at[0, :192].set(b_gate)
    ng = norm_g.reshape(1, GLA_WIDTH)
    pw = jnp.zeros((POOL_WIDTH, POOL_WIDTH), F32)
    for gi in range(len(POOL_WINDOWS)):
        sl = slice(gi * POOL_GROUP, (gi + 1) * POOL_GROUP)
        pw = pw.at[sl, sl].set(pool_w[gi])
    return dict(wp=wp, pos2=pos2, w1bd=w1bd, w2bd=w2bd, wg=wg, bg=bg, ng=ng, pw=pw.astype(BF16),
                ps=pool_scale.reshape(1, POOL_WIDTH))


@jax.jit
def kernel(x, w_in, cmp_pos, cmp_w1, cmp_w2, gla_w_gate2, gla_b_gate, gla_norm_g, pool_w, pool_scale,
           w_out, ln1_g, ln1_b, w_up, w_down, ln2_g, ln2_b):
    batch, seq, d = x.shape
    assert d == D_MODEL and seq % GLA_TILE == 0 and seq >= WIN_SPAN and seq // SEL_BLOCK <= LANE
    tm = 512
    st = _static_tables(seq)
    rope_q = _rope_tables(seq, HEAD_DIM ** -0.5 * LOG2E)
    rope_k = _rope_tables(seq, 1.0)
    h = x.reshape(batch * seq, d)
    for l in range(w_in.shape[0]):
        lw = _layer_weights(w_in[l], cmp_pos[l], cmp_w1[l], cmp_w2[l], gla_w_gate2[l], gla_b_gate[l],
                            gla_norm_g[l], pool_w[l], pool_scale[l])
        q, cmp_k, cmp_v, sel, win, small, gq, gk, gv, og, pu = _proj(h, lw["wp"], rope_q, rope_k, seq, tm)
        kc = _compress(cmp_k, cmp_v, lw["pos2"], lw["w1bd"], lw["w2bd"], batch, seq)
        nsa = _nsa(q, kc, sel, win, small, st["onehot"], st["c2s_t"], st["eye"], st["perm"], batch, seq)
        gla = _gla(gq, gk, gv, og, small, lw["wg"], lw["bg"], lw["ng"], st["ltri"], st["lsum"], st["bd"],
                   batch, seq)
        pool = _pool(pu, st["band_cur"], st["band_prev"], lw["pw"], lw["ps"], batch, seq)
        h = _mix_ffn(nsa, gla, pool, h, w_out[l].astype(BF16), ln1_g[l].reshape(1, d), ln1_b[l].reshape(1, d),
                     w_up[l].astype(BF16), w_down[l].astype(BF16), ln2_g[l].reshape(1, d), ln2_b[l].reshape(1, d), tm)
    return h.reshape(batch, seq, d)
```

```python
import functools
import numpy as np
import jax
import jax.numpy as jnp
from jax import lax
from jax.experimental import pallas as pl
from jax.experimental.pallas import tpu as pltpu

F32 = jnp.float32
BF16 = jnp.bfloat16

D_MODEL = 1024
HEAD_DIM = 64
NSA_G = 2
NSA_H = 3
NSA_WIDTH = NSA_G * NSA_H * HEAD_DIM
CMP_BLOCK = 32
CMP_STRIDE = 16
SEL_BLOCK = 64
SEL_TOP_N = 16
WINDOW = 512
FORCE_SCORE = 1.0e4
GLA_HEADS = 6
GLA_DK = 32
GLA_DV = 64
GLA_WIDTH = GLA_HEADS * GLA_DV
GLA_RANK = 16
GLA_GATE_NORM = 16.0
GLA_CHUNK = 64
GLA_QK_PAD = 256
POOL_WINDOWS = (2, 4, 8, 16)
POOL_GROUP = 64
POOL_WIDTH = 256
D_FF = 4 * D_MODEL
ROPE_THETA = 500000.0
ROPE_DIM = HEAD_DIM // 4
LN_EPS = 1e-5
RMS_EPS = 1e-6
DEPTH = 4
ALPHA = (2 * DEPTH) ** 0.25

LANE = 128
NEG = -1e30
SEL_NEG = -30000.0

VMEM_LIMIT = 48 * 1024 * 1024

SMALL_GATE0 = 0
SMALL_GLR0 = 32
PROJ_COLS = 22 * LANE


def _nt(a, b):
    return lax.dot_general(a, b, (((1,), (1,)), ((), ())), preferred_element_type=F32)


def _tn(a, b):
    return lax.dot_general(a, b, (((0,), (0,)), ((), ())), preferred_element_type=F32)


def _dot(a, b):
    return jnp.dot(a, b, preferred_element_type=F32)


def _split3(x):
    hi = x.astype(BF16)
    r = x - hi.astype(F32)
    mid = r.astype(BF16)
    lo = (r - mid.astype(F32)).astype(BF16)
    return hi, mid, lo


def _layer_norm(y, g, b):
    mu = jnp.mean(y, axis=-1, keepdims=True)
    d = y - mu
    var = jnp.mean(d * d, axis=-1, keepdims=True)
    return d * lax.rsqrt(var + LN_EPS) * g + b


def _rope(x, a, bm, cm):
    return x * a + pltpu.roll(x, LANE - ROPE_DIM // 2, 1) * bm + pltpu.roll(x, ROPE_DIM // 2, 1) * cm


def _proj_kernel(h_ref, w_ref, qa_ref, qb_ref, qc_ref, ka_ref, kb_ref, kc_ref,
                 q_out, cmpk_out, cmpv_out, sel_out, win_out, small_out, gq_out, gk_out, gv_out, og_out, pu_out):
    hb = h_ref[...].astype(BF16)
    plan = ([(q_out, j, "q") for j in range(3)]
            + [(cmpk_out, 0, "k"), (cmpv_out, 0, None), (sel_out, 0, "k"), (sel_out, 1, None),
               (win_out, 0, "k"), (win_out, 1, None), (small_out, 0, None)]
            + [(gq_out, j, None) for j in range(2)] + [(gk_out, j, None) for j in range(2)]
            + [(gv_out, j, None) for j in range(3)] + [(og_out, j, None) for j in range(3)]
            + [(pu_out, j, None) for j in range(2)])
    for c in range(0, len(plan), 2):
        acc = _dot(hb, w_ref[:, c * LANE:(c + 2) * LANE])
        for half in range(2):
            ref, j, kind = plan[c + half]
            x = acc[:, half * LANE:(half + 1) * LANE]
            if kind == "q":
                x = _rope(x, qa_ref[...], qb_ref[...], qc_ref[...])
            elif kind == "k":
                x = _rope(x, ka_ref[...], kb_ref[...], kc_ref[...])
            ref[:, j * LANE:(j + 1) * LANE] = x.astype(ref.dtype)


def _proj(h, w, rope_q, rope_k, seq, tm):
    m = h.shape[0]
    tiles_per_seq = seq // tm
    row = lambda i: (i, 0)
    tab = lambda i: (i % tiles_per_seq, 0)
    widths = [(NSA_WIDTH, BF16), (LANE, F32), (LANE, F32), (256, BF16), (256, BF16), (LANE, F32),
              (GLA_QK_PAD, BF16), (GLA_QK_PAD, BF16), (GLA_WIDTH, BF16), (GLA_WIDTH, BF16), (POOL_WIDTH, BF16)]
    return pl.pallas_call(
        _proj_kernel,
        grid=(m // tm,),
        in_specs=[pl.BlockSpec((tm, D_MODEL), row),
                  pl.BlockSpec((D_MODEL, PROJ_COLS), lambda i: (0, 0), pipeline_mode=pl.Buffered(1))]
                 + [pl.BlockSpec((tm, LANE), tab)] * 6,
        out_specs=[pl.BlockSpec((tm, wd), row) for wd, _ in widths],
        out_shape=[jax.ShapeDtypeStruct((m, wd), dt) for wd, dt in widths],
        compiler_params=pltpu.CompilerParams(dimension_semantics=("parallel",), vmem_limit_bytes=VMEM_LIMIT),
        name="in_proj",
    )(h, w, *rope_q, *rope_k)


def _cmp_kernel(k_ref, v_ref, pos_ref, w1_ref, w2_ref, o_ref):
    nblk = o_ref.shape[0]
    half = CMP_BLOCK // 2
    for kvi, x_ref in enumerate((k_ref, v_ref)):
        lanes = slice(kvi * LANE, (kvi + 1) * LANE)
        ha = jnp.zeros((nblk, LANE), F32)
        hb = jnp.zeros((nblk, LANE), F32)
        for r in range(half):
            t = x_ref[pl.ds(r, nblk, stride=CMP_STRIDE), :]
            ha = ha + _dot((t + pos_ref[kvi, r:r + 1, :]).astype(BF16), w1_ref[kvi, r])
            hb = hb + _dot((t + pos_ref[kvi, half + r:half + r + 1, :]).astype(BF16), w1_ref[kvi, half + r])
        hid = ha + pltpu.roll(hb, nblk - 1, 0)
        hid = jax.nn.gelu(hid, approximate=True)
        o_ref[:, lanes] = _dot(hid.astype(BF16), w2_ref[kvi])


def _compress(cmp_k, cmp_v, pos2, w1bd, w2bd, batch, seq):
    nblk = seq // CMP_STRIDE
    return pl.pallas_call(
        _cmp_kernel,
        grid=(batch,),
        in_specs=[pl.BlockSpec((seq, LANE), lambda b: (b, 0)), pl.BlockSpec((seq, LANE), lambda b: (b, 0)),
                  pl.BlockSpec((2, CMP_BLOCK, LANE), lambda b: (0, 0, 0)),
                  pl.BlockSpec((2, CMP_BLOCK, LANE, LANE), lambda b: (0, 0, 0, 0)),
                  pl.BlockSpec((2, LANE, LANE), lambda b: (0, 0, 0))],
        out_specs=pl.BlockSpec((nblk, 256), lambda b: (b, 0)),
        out_shape=jax.ShapeDtypeStruct((batch * nblk, 256), F32),
        compiler_params=pltpu.CompilerParams(dimension_semantics=("parallel",), vmem_limit_bytes=VMEM_LIMIT),
        name="nsa_compress",
    )(cmp_k, cmp_v, pos2, w1bd, w2bd)


NSA_TQ = 256
NSA_TK = 512
WIN_SPAN = WINDOW + NSA_TQ
LOG2E = 1.4426950408889634


def _nsa_kernel(q_ref, kc_ref, sel_ref, win_ref, oh_ref, g_ref, c2s_ref, eye_ref, perm_ref, o_ref,
                kaug, vsel, vwin, vcmp, qaug, gate_scr, mix_scr):
    tq, tk = NSA_TQ, NSA_TK
    rows3 = NSA_H * tq
    qi = pl.program_id(1)
    n_sel = oh_ref.shape[0] // SEL_BLOCK
    lane2 = lax.broadcasted_iota(jnp.int32, (1, 2 * LANE), 1)
    ones_half = jnp.where(lane2 >= LANE, 1.0, 0.0)

    @pl.when(qi == 0)
    def _():
        kaug[:, 0:LANE] = sel_ref[:, 0:LANE]
        kaug[:, LANE:2 * LANE] = oh_ref[...]
        for g in range(NSA_G):
            vsel[g] = (_dot(sel_ref[:, LANE:2 * LANE], perm_ref[g]) + ones_half).astype(BF16)
            vwin[g] = (_dot(win_ref[:, LANE:2 * LANE], perm_ref[g]) + ones_half).astype(BF16)
            vcmp[g] = _dot(kc_ref[:, LANE:2 * LANE].astype(BF16), perm_ref[g, :, 0:LANE]).astype(BF16)

    q0 = qi * tq
    t3 = q0 + (lax.broadcasted_iota(jnp.int32, (rows3, 1), 0) & (tq - 1))
    t_lane = q0 + lax.broadcasted_iota(jnp.int32, (1, tq), 1)
    gates = jax.nn.sigmoid(g_ref[...])
    for g in range(NSA_G):
        for h in range(NSA_H):
            for br in range(3):
                col = (g * NSA_H + h) * 3 + br
                gate_scr[g, br, h * tq:(h + 1) * tq, :] = jnp.broadcast_to(gates[:, col:col + 1], (tq, LANE))
    kc_b = kc_ref[:, 0:LANE].astype(BF16)
    for g in range(NSA_G):
        glanes = slice(g * HEAD_DIM, (g + 1) * HEAD_DIM)
        qaug[g, :, 0:LANE] = jnp.zeros((rows3, LANE), BF16)
        for h in range(NSA_H):
            c = (NSA_H * g + h) * HEAD_DIM
            qaug[g, h * tq:(h + 1) * tq, glanes] = q_ref[:, c:c + HEAD_DIM]

    ks = pl.multiple_of(jnp.maximum(q0 - WINDOW, 0), tq)
    kw = win_ref[pl.ds(ks, WIN_SPAN), 0:LANE]
    t1 = q0 + lax.broadcasted_iota(jnp.int32, (tq, 1), 0)
    diff = t1 - (ks + lax.broadcasted_iota(jnp.int32, (1, WIN_SPAN), 1))
    wbias = jnp.where((diff >= 0) & (diff < WINDOW), 0.0, NEG)
    wbias = jnp.concatenate([wbias] * NSA_H, axis=0)
    sws = [_nt(qaug[g, :, 0:LANE], kw) + wbias for g in range(NSA_G)]
    scs = [_nt(qaug[g, :, 0:LANE], kc_b) for g in range(NSA_G)]
    pws = [jnp.exp2(sw - jnp.max(sw, axis=1, keepdims=True)).astype(BF16) for sw in sws]
    a_wins = [_dot(pws[g], vwin[g, pl.ds(ks, WIN_SPAN), :]) for g in range(NSA_G)]
    o_win = [a[:, 0:LANE] * (gate_scr[g, 2] / a[:, LANE:2 * LANE]) for g, a in enumerate(a_wins)]

    for g in range(NSA_G):
        s = scs[g]
        ncol = lax.broadcasted_iota(jnp.int32, (1, s.shape[1]), 1)
        cmask = (CMP_STRIDE * ncol + (CMP_BLOCK - 1)) <= t3
        sm = jnp.where(cmask, s, NEG)
        m = jnp.max(sm, axis=1, keepdims=True)
        e = jnp.where(cmask, jnp.exp2(sm - m), 0.0)
        p = e / jnp.maximum(jnp.sum(e, axis=1, keepdims=True), 1e-30)
        o_cmp = _dot(p.astype(BF16), vcmp[g])
        mix_scr[g] = gate_scr[g, 0] * o_cmp + o_win[g]

        psum = p[0:tq] + p[tq:2 * tq] + p[2 * tq:3 * tq]
        c2s = c2s_ref[...]
        imp = sum(_nt(c2s, term) for term in _split3(psum))
        jb = lax.broadcasted_iota(jnp.int32, (n_sel, 1), 0)
        cur = t_lane // SEL_BLOCK
        imp = jnp.where((jb == 0) | (jb == cur) | (jb == cur - 1), FORCE_SCORE, imp)
        imp = jnp.where(jb > cur, -FORCE_SCORE, imp)
        rank = jnp.zeros(imp.shape, F32)
        for i in range(n_sel):
            ri = imp[i:i + 1, :]
            ahead = (ri > imp) | ((ri == imp) & (jb > i))
            rank = rank + jnp.where(ahead, 1.0, 0.0)
        sel_t = jnp.where(rank < float(min(SEL_TOP_N, n_sel)), 1.0, 0.0).astype(BF16)
        sel_t = jnp.concatenate([sel_t, jnp.zeros((LANE - n_sel, tq), BF16)], axis=0)
        sel = _nt(eye_ref[...], sel_t)
        bias = ((sel - 1.0) * (-SEL_NEG)).astype(BF16)
        for h in range(NSA_H):
            qaug[g, h * tq:(h + 1) * tq, LANE:2 * LANE] = bias

    last = (q0 + tq - 1) // tk

    def selected(n_chunks):
        lo = (n_chunks - 1) * tk
        kpos = lo + lax.broadcasted_iota(jnp.int32, (1, tk), 1)
        cbias = jnp.concatenate([jnp.where(kpos <= t1, 0.0, NEG)] * NSA_H, axis=0)
        heads = []
        sbs = [_nt(qaug[g], kaug[lo:lo + tk, :]) + cbias for g in range(NSA_G)]
        sas = [_nt(qaug[g], kaug[0:lo, :]) for g in range(NSA_G)] if lo else None
        for g in range(NSA_G):
            sb = sbs[g]
            blocks = [sb[:, j * LANE:(j + 1) * LANE] for j in range(tk // LANE)]
            if lo:
                sa = sas[g]
                blocks += [sa[:, j * LANE:(j + 1) * LANE] for j in range(lo // LANE)]
            m = jnp.max(functools.reduce(jnp.maximum, blocks), axis=1, keepdims=True)
            acc = _dot(jnp.exp2(sb - m).astype(BF16), vsel[g, lo:lo + tk, :])
            if lo:
                acc = acc + _dot(jnp.exp2(sa - m).astype(BF16), vsel[g, 0:lo, :])
            mix = mix_scr[g] + acc[:, 0:LANE] * (gate_scr[g, 1] / acc[:, LANE:2 * LANE])
            heads += [mix[h * tq:(h + 1) * tq] for h in range(NSA_H)]
        low = lax.broadcasted_iota(jnp.int32, (1, LANE), 1) < HEAD_DIM
        for j in range(len(heads) // 2):
            pair = jnp.where(low, heads[2 * j], heads[2 * j + 1])
            o_ref[:, j * LANE:(j + 1) * LANE] = pair.astype(o_ref.dtype)

    for n_chunks in range(1, kaug.shape[0] // tk + 1):
        pl.when(last == n_chunks - 1)(functools.partial(selected, n_chunks))


def _nsa(q, kc, sel, win, small, onehot, c2s_t, eye, perm, batch, seq):
    nq = seq // NSA_TQ
    nblk = seq // CMP_STRIDE
    rows3 = NSA_H * NSA_TQ
    qrow = lambda b, i: (b * nq + i, 0)
    per_b = lambda b, i: (b, 0)
    const = lambda b, i: (0, 0)
    return pl.pallas_call(
        _nsa_kernel,
        grid=(batch, nq),
        in_specs=[pl.BlockSpec((NSA_TQ, NSA_WIDTH), qrow),
                  pl.BlockSpec((nblk, 256), per_b),
                  pl.BlockSpec((seq, 256), per_b),
                  pl.BlockSpec((seq, 256), per_b),
                  pl.BlockSpec((seq, LANE), const),
                  pl.BlockSpec((NSA_TQ, LANE), qrow),
                  pl.BlockSpec(c2s_t.shape, const),
                  pl.BlockSpec((NSA_TQ, NSA_TQ), const),
                  pl.BlockSpec(perm.shape, lambda b, i: (0, 0, 0))],
        out_specs=pl.BlockSpec((NSA_TQ, NSA_WIDTH), qrow),
        out_shape=jax.ShapeDtypeStruct((batch * seq, NSA_WIDTH), BF16),
        scratch_shapes=[pltpu.VMEM((seq, 2 * LANE), BF16),
                        pltpu.VMEM((NSA_G, seq, 2 * LANE), BF16),
                        pltpu.VMEM((NSA_G, seq, 2 * LANE), BF16),
                        pltpu.VMEM((NSA_G, nblk, LANE), BF16),
                        pltpu.VMEM((NSA_G, rows3, 2 * LANE), BF16),
                        pltpu.VMEM((NSA_G, 3, rows3, LANE), F32),
                        pltpu.VMEM((NSA_G, rows3, LANE), F32)],
        compiler_params=pltpu.CompilerParams(dimension_semantics=("parallel", "arbitrary"),
                                             vmem_limit_bytes=VMEM_LIMIT),
        name="nsa_attention",
    )(q, kc, sel, win, onehot, small, c2s_t, eye, perm)


GLA_TILE = 256


def _log_sigmoid(x):
    return jnp.minimum(x, 0.0) - jnp.log(1.0 + jnp.exp(-jnp.abs(x)))


def _gla_kernel(q_ref, k_ref, v_ref, og_ref, small_ref, wg_ref, bg_ref, ng_ref, ltri_ref, lsum_ref, bd_ref,
                o_ref, qd_s, ke_s, dec_s, oacc_s, st_s):
    seq = q_ref.shape[0]
    ch = GLA_CHUNK
    nh = GLA_HEADS
    tile = GLA_TILE
    qk_head = lax.broadcasted_iota(jnp.int32, (1, GLA_QK_PAD), 1) // GLA_DK
    low = lax.broadcasted_iota(jnp.int32, (1, LANE), 1) < GLA_DV
    ri = lax.broadcasted_iota(jnp.int32, (tile, 1), 0)
    cj = lax.broadcasted_iota(jnp.int32, (1, tile), 1)
    intra = (cj <= ri) & ((cj // ch) == (ri // ch))
    intra = jnp.concatenate([intra] * nh, axis=1)

    def pre(i, _):
        for half in range(2):
            pre_tile(2 * i + half)
        return 0

    def pre_tile(i):
        rs = pl.ds(pl.multiple_of(i * tile, tile), tile)
        x_hi, x_mid, _ = _split3(small_ref[rs, :])
        w_hi, w_mid, _ = _split3(wg_ref[...])
        z = _dot(x_hi, w_hi) + (_dot(x_hi, w_mid) + _dot(x_mid, w_hi)) + bg_ref[...]
        gl = _log_sigmoid(z) * (1.0 / GLA_GATE_NORM)
        g_hi, g_mid, _ = _split3(gl)
        bc = _dot(ltri_ref[...], g_hi) + _dot(ltri_ref[...], g_mid)
        bl = _dot(lsum_ref[...], g_hi) + _dot(lsum_ref[...], g_mid)
        dec = jnp.exp(bl)
        qd = (q_ref[rs, :].astype(F32) * (GLA_DK ** -0.5) * jnp.exp(bc)).astype(BF16)
        k_inv = k_ref[rs, :].astype(F32) * jnp.exp(-bc)
        ki = k_inv.astype(BF16)
        qd_s[rs, :] = qd
        ke_s[rs, :] = (k_inv * dec).astype(BF16)
        dec_s[rs, :] = dec
        k_heads = jnp.concatenate([jnp.where(qk_head == h, ki, jnp.zeros_like(ki)) for h in range(nh)], axis=0)
        a = jnp.where(intra, _nt(qd, k_heads), 0.0).astype(BF16)
        for pr in range(nh // 2):
            lanes = slice(pr * LANE, (pr + 1) * LANE)
            vp = v_ref[rs, lanes]
            zero = jnp.zeros_like(vp)
            v_pair = jnp.concatenate([jnp.where(low, vp, zero), jnp.where(low, zero, vp)], axis=0)
            oacc_s[rs, lanes] = _dot(a[:, 2 * pr * tile:(2 * pr + 2) * tile], v_pair)

    lax.fori_loop(0, seq // (2 * tile), pre, 0)

    st_mask = (lax.broadcasted_iota(jnp.int32, (GLA_WIDTH, 1), 0) // GLA_DV) == qk_head
    st_s[...] = jnp.zeros(st_s.shape, F32)
    per_iter = tile // ch

    def scan(i, _):
        st = st_s[...]
        base = pl.multiple_of(i * tile, tile)
        dts = [_tn(v_ref[pl.ds(base + c * ch, ch), :], ke_s[pl.ds(base + c * ch, ch), :]) for c in range(per_iter)]
        for c in range(per_iter):
            rs = pl.ds(base + c * ch, ch)
            oacc_s[rs, :] = oacc_s[rs, :] + _nt(qd_s[rs, :], st.astype(BF16))
            st = st * dec_s[pl.ds(base + c * ch, 1), :] + jnp.where(st_mask, dts[c], 0.0)
        st_s[...] = st
        return 0

    lax.fori_loop(0, seq // tile, scan, 0)

    def post(i, _):
        rs = pl.ds(pl.multiple_of(i * tile, tile), tile)
        o = oacc_s[rs, :]
        ms = _dot((o * o).astype(BF16), bd_ref[...])
        og = og_ref[rs, :].astype(F32)
        y = o * lax.rsqrt(ms + RMS_EPS) * ng_ref[...] * (og * jax.nn.sigmoid(og))
        o_ref[rs, :] = y.astype(o_ref.dtype)
        return 0

    lax.fori_loop(0, seq // tile, post, 0)


def _gla(gq, gk, gv, og, small, wg, bg, ng, ltri, lsum, bd, batch, seq):
    per_b = lambda b: (b, 0)
    const = lambda b: (0, 0)
    return pl.pallas_call(
        _gla_kernel,
        grid=(batch,),
        in_specs=[pl.BlockSpec((seq, GLA_QK_PAD), per_b), pl.BlockSpec((seq, GLA_QK_PAD), per_b),
                  pl.BlockSpec((seq, GLA_WIDTH), per_b), pl.BlockSpec((seq, GLA_WIDTH), per_b),
                  pl.BlockSpec((seq, LANE), per_b),
                  pl.BlockSpec(wg.shape, const), pl.BlockSpec(bg.shape, const), pl.BlockSpec(ng.shape, const),
                  pl.BlockSpec(ltri.shape, const), pl.BlockSpec(lsum.shape, const), pl.BlockSpec(bd.shape, const)],
        out_specs=pl.BlockSpec((seq, GLA_WIDTH), per_b),
        out_shape=jax.ShapeDtypeStruct((batch * seq, GLA_WIDTH), BF16),
        scratch_shapes=[pltpu.VMEM((seq, GLA_QK_PAD), BF16), pltpu.VMEM((seq, GLA_QK_PAD), BF16),
                        pltpu.VMEM((seq, GLA_QK_PAD), F32),
                        pltpu.VMEM((seq, GLA_WIDTH), F32), pltpu.VMEM((GLA_WIDTH, GLA_QK_PAD), F32)],
        compiler_params=pltpu.CompilerParams(dimension_semantics=("parallel",), vmem_limit_bytes=VMEM_LIMIT),
        name="gla",
    )(gq, gk, gv, og, small, wg, bg, ng, ltri, lsum, bd)


POOL_TILE = 256
POOL_HALO = 16


def _pool_kernel(u_ref, w_ref, scale_ref, o_ref):
    seq = u_ref.shape[0]
    lane = lax.broadcasted_iota(jnp.int32, (1, POOL_WIDTH), 1)
    grp = lane // POOL_GROUP
    win = jnp.where(grp == 0, POOL_WINDOWS[0],
                    jnp.where(grp == 1, POOL_WINDOWS[1], jnp.where(grp == 2, POOL_WINDOWS[2], POOL_WINDOWS[3])))
    for i in range(seq // POOL_TILE):
        r0 = i * POOL_TILE
        u = u_ref[r0:r0 + POOL_TILE, :].astype(F32)
        if i > 0:
            x = u_ref[r0 - POOL_HALO:r0 + POOL_TILE, :].astype(F32)
        else:
            x = jnp.concatenate([jnp.zeros((POOL_HALO, POOL_WIDTH), F32), u], axis=0)
        tot = x
        span = 1
        for gi, w in enumerate(POOL_WINDOWS):
            while span < w:
                x = x + pltpu.roll(x, span, 0)
                span *= 2
            tot = jnp.where(grp == gi, x, tot)
        tot = tot[POOL_HALO:, :]
        t = r0 + lax.broadcasted_iota(jnp.int32, (POOL_TILE, 1), 0)
        cnt = jnp.minimum(t + 1, win).astype(F32)
        pooled = tot / cnt - u
        o_ref[r0:r0 + POOL_TILE, :] = (_dot(pooled.astype(BF16), w_ref[...]) * scale_ref[...]).astype(o_ref.dtype)


def _pool(u, wbd, scale, batch, seq):
    per_b = lambda b: (b, 0)
    const = lambda b: (0, 0)
    return pl.pallas_call(
        _pool_kernel,
        grid=(batch,),
        in_specs=[pl.BlockSpec((seq, POOL_WIDTH), per_b),
                  pl.BlockSpec(wbd.shape, const), pl.BlockSpec(scale.shape, const)],
        out_specs=pl.BlockSpec((seq, POOL_WIDTH), per_b),
        out_shape=jax.ShapeDtypeStruct((batch * seq, POOL_WIDTH), BF16),
        compiler_params=pltpu.CompilerParams(dimension_semantics=("parallel",), vmem_limit_bytes=VMEM_LIMIT),
        name="pool",
    )(u, wbd, scale)


FFN_CHUNK = 1024


def _mix_ffn_kernel(nsa_ref, gla_ref, pool_ref, h_ref, wo_ref, g1_ref, b1_ref, wu_ref, wd_ref, g2_ref, b2_ref,
                    o_ref):
    acc = _dot(nsa_ref[...], wo_ref[0:NSA_WIDTH, :])
    acc = acc + _dot(gla_ref[...], wo_ref[NSA_WIDTH:NSA_WIDTH + GLA_WIDTH, :])
    acc = acc + _dot(pool_ref[...], wo_ref[NSA_WIDTH + GLA_WIDTH:, :])
    h = _layer_norm(ALPHA * h_ref[...] + acc, g1_ref[...], b1_ref[...])
    hb = h.astype(BF16)
    acc = ALPHA * h
    for c in range(D_FF // FFN_CHUNK):
        cs = slice(c * FFN_CHUNK, (c + 1) * FFN_CHUNK)
        u = jnp.maximum(_dot(hb, wu_ref[:, cs]), 0.0)
        acc = acc + _dot((u * u).astype(BF16), wd_ref[cs, :])
    o_ref[...] = _layer_norm(acc, g2_ref[...], b2_ref[...])


def _mix_ffn(nsa, gla, pool, h, wo, g1, b1, wu, wd, g2, b2, layer, tm):
    m = h.shape[0]
    row = lambda i: (i, 0)
    const = lambda i: (0, 0)
    vec = pl.BlockSpec((1, D_MODEL), const)
    resident = lambda a: pl.BlockSpec((None,) + a.shape[1:], lambda i: (layer, 0, 0), pipeline_mode=pl.Buffered(1))
    return pl.pallas_call(
        _mix_ffn_kernel,
        grid=(m // tm,),
        in_specs=[pl.BlockSpec((tm, NSA_WIDTH), row), pl.BlockSpec((tm, GLA_WIDTH), row),
                  pl.BlockSpec((tm, POOL_WIDTH), row), pl.BlockSpec((tm, D_MODEL), row),
                  resident(wo), vec, vec, resident(wu), resident(wd), vec, vec],
        out_specs=pl.BlockSpec((tm, D_MODEL), row),
        out_shape=jax.ShapeDtypeStruct((m, D_MODEL), F32),
        compiler_params=pltpu.CompilerParams(dimension_semantics=("parallel",), vmem_limit_bytes=VMEM_LIMIT),
        name="out_proj_ffn",
    )(nsa, gla, pool, h, wo, g1, b1, wu, wd, g2, b2)


def _rope_tables(seq, scale):
    half = ROPE_DIM // 2
    inv = ROPE_THETA ** (-jnp.arange(half, dtype=F32) * 2.0 / ROPE_DIM)
    ang = jnp.arange(seq).astype(F32)[:, None] * inv[None, :]
    cos, sin = jnp.cos(ang), jnp.sin(ang)
    ones = jnp.ones((seq, HEAD_DIM - ROPE_DIM), F32)
    zeros = jnp.zeros((seq, HEAD_DIM - ROPE_DIM), F32)
    zh = jnp.zeros((seq, half), F32)
    a = jnp.concatenate([cos, cos, ones], axis=1)
    bm = jnp.concatenate([-sin, zh, zeros], axis=1)
    cm = jnp.concatenate([zh, sin, zeros], axis=1)
    reps = LANE // HEAD_DIM
    return tuple(jnp.tile(t, (1, reps)) * scale for t in (a, bm, cm))


def _static_tables(seq):
    n_sel = seq // SEL_BLOCK
    nblk = seq // CMP_STRIDE
    n_cmp = (seq - CMP_BLOCK) // CMP_STRIDE + 1
    onehot = np.zeros((seq, LANE), np.float32)
    onehot[np.arange(seq), np.arange(seq) // SEL_BLOCK] = 1.0
    c_lo = CMP_STRIDE * np.arange(n_cmp)[:, None]
    s_lo = SEL_BLOCK * np.arange(n_sel)[None, :]
    overlap = np.clip(np.minimum(c_lo + CMP_BLOCK, s_lo + SEL_BLOCK) - np.maximum(c_lo, s_lo), 0, None)
    c2s_t = np.zeros((n_sel, nblk), np.float32)
    c2s_t[:, :n_cmp] = (overlap.astype(np.float32) / CMP_BLOCK).T
    eye = np.eye(NSA_TQ, dtype=np.float32)
    perm = np.zeros((NSA_G, LANE, 2 * LANE), np.float32)
    for g in range(NSA_G):
        for rep in range(2):
            perm[g, g * HEAD_DIM + np.arange(HEAD_DIM), rep * HEAD_DIM + np.arange(HEAD_DIM)] = 1.0
    r = np.arange(GLA_TILE)
    same = (r[:, None] // GLA_CHUNK) == (r[None, :] // GLA_CHUNK)
    ltri = (same & (r[None, :] <= r[:, None])).astype(np.float32)
    lsum = same.astype(np.float32)
    hv = np.arange(GLA_WIDTH) // GLA_DV
    bd = (hv[:, None] == hv[None, :]).astype(np.float32) / GLA_DV
    as_bf = lambda a: jnp.asarray(a, BF16)
    return dict(onehot=as_bf(onehot), c2s_t=as_bf(c2s_t), eye=as_bf(eye), perm=as_bf(perm), ltri=as_bf(ltri),
                lsum=as_bf(lsum), bd=as_bf(bd))


def _block_diag2(w):
    z = jnp.zeros_like(w)
    return jnp.concatenate([jnp.concatenate([w, z], axis=-1), jnp.concatenate([z, w], axis=-1)], axis=-2)


def _layer_weights(w_in, cmp_pos, cmp_w1, cmp_w2, w_gate2, b_gate, norm_g, pool_w, pool_scale):
    o_nq, o_nkv, o_gate = 0, NSA_WIDTH, NSA_WIDTH + 768
    o_gq = o_gate + 18
    o_gk = o_gq + 192
    o_gv = o_gk + 192
    o_glr = o_gv + GLA_WIDTH
    o_og = o_glr + GLA_RANK
    o_pu = o_og + GLA_WIDTH
    zc = lambda n: jnp.zeros((D_MODEL, n), F32)
    small = jnp.concatenate([w_in[:, o_gate:o_gq], zc(SMALL_GLR0 - 18), w_in[:, o_glr:o_og],
                             zc(LANE - SMALL_GLR0 - GLA_RANK)], axis=1)
    wp = jnp.concatenate([w_in[:, o_nq:o_gate], small,
                          w_in[:, o_gq:o_gk], zc(GLA_QK_PAD - 192), w_in[:, o_gk:o_gv], zc(GLA_QK_PAD - 192),
                          w_in[:, o_gv:o_glr], w_in[:, o_og:o_pu], w_in[:, o_pu:]], axis=1).astype(BF16)
    pos2 = jnp.concatenate([cmp_pos, cmp_pos], axis=-1)
    w1bd = _block_diag2(cmp_w1.reshape(2, CMP_BLOCK, HEAD_DIM, cmp_w1.shape[-1])).astype(BF16)
    w2bd = _block_diag2(cmp_w2).astype(BF16)
    wg = jnp.zeros((LANE, GLA_QK_PAD), F32).at[SMALL_GLR0:SMALL_GLR0 + GLA_RANK, :192].set(w_gate2)
    bg = jnp.zeros((1, GLA_QK_PAD), F32).at[0, :192].set(b_gate)
    ng = norm_g.reshape(1, GLA_WIDTH)
    pw = jnp.zeros((POOL_WIDTH, POOL_WIDTH), F32)
    for gi in range(len(POOL_WINDOWS)):
        sl = slice(gi * POOL_GROUP, (gi + 1) * POOL_GROUP)
        pw = pw.at[sl, sl].set(pool_w[gi])
    return dict(wp=wp, pos2=pos2, w1bd=w1bd, w2bd=w2bd, wg=wg, bg=bg, ng=ng, pw=pw.astype(BF16),
                ps=pool_scale.reshape(1, POOL_WIDTH))


@jax.jit
def kernel(x, w_in, cmp_pos, cmp_w1, cmp_w2, gla_w_gate2, gla_b_gate, gla_norm_g, pool_w, pool_scale,
           w_out, ln1_g, ln1_b, w_up, w_down, ln2_g, ln2_b):
    batch, seq, d = x.shape
    assert d == D_MODEL and seq % GLA_TILE == 0 and seq >= WIN_SPAN and seq // SEL_BLOCK <= LANE
    tm = 512
    st = _static_tables(seq)
    rope_q = _rope_tables(seq, HEAD_DIM ** -0.5 * LOG2E)
    rope_k = _rope_tables(seq, 1.0)
    wo_all, wu_all, wd_all = w_out.astype(BF16), w_up.astype(BF16), w_down.astype(BF16)
    h = x.reshape(batch * seq, d)
    for l in range(w_in.shape[0]):
        lw = _layer_weights(w_in[l], cmp_pos[l], cmp_w1[l], cmp_w2[l], gla_w_gate2[l], gla_b_gate[l],
                            gla_norm_g[l], pool_w[l], pool_scale[l])
        q, cmp_k, cmp_v, sel, win, small, gq, gk, gv, og, pu = _proj(h, lw["wp"], rope_q, rope_k, seq, tm)
        kc = _compress(cmp_k, cmp_v, lw["pos2"], lw["w1bd"], lw["w2bd"], batch, seq)
        nsa = _nsa(q, kc, sel, win, small, st["onehot"], st["c2s_t"], st["eye"], st["perm"], batch, seq)
        gla = _gla(gq, gk, gv, og, small, lw["wg"], lw["bg"], lw["ng"], st["ltri"], st["lsum"], st["bd"],
                   batch, seq)
        pool = _pool(pu, lw["pw"], lw["ps"], batch, seq)
        h = _mix_ffn(nsa, gla, pool, h, wo_all, ln1_g[l].reshape(1, d), ln1_b[l].reshape(1, d),
                     wu_all, wd_all, ln2_g[l].reshape(1, d), ln2_b[l].reshape(1, d), l, tm)
    return h.reshape(batch, seq, d)
```

```python
import functools
import numpy as np
import jax
import jax.numpy as jnp
from jax import lax
from jax.experimental import pallas as pl
from jax.experimental.pallas import tpu as pltpu

F32 = jnp.float32
BF16 = jnp.bfloat16

D_MODEL = 1024
HEAD_DIM = 64
NSA_G = 2
NSA_H = 3
NSA_WIDTH = NSA_G * NSA_H * HEAD_DIM
CMP_BLOCK = 32
CMP_STRIDE = 16
SEL_BLOCK = 64
SEL_TOP_N = 16
WINDOW = 512
FORCE_SCORE = 1.0e4
GLA_HEADS = 6
GLA_DK = 32
GLA_DV = 64
GLA_WIDTH = GLA_HEADS * GLA_DV
GLA_RANK = 16
GLA_GATE_NORM = 16.0
GLA_CHUNK = 64
GLA_QK_PAD = 256
POOL_WINDOWS = (2, 4, 8, 16)
POOL_GROUP = 64
POOL_WIDTH = 256
D_FF = 4 * D_MODEL
ROPE_THETA = 500000.0
ROPE_DIM = HEAD_DIM // 4
LN_EPS = 1e-5
RMS_EPS = 1e-6
DEPTH = 4
ALPHA = (2 * DEPTH) ** 0.25

LANE = 128
NEG = -1e30
SEL_NEG = -30000.0

VMEM_LIMIT = 48 * 1024 * 1024
FFN_VMEM_LIMIT = 54 * 1024 * 1024

SMALL_GATE0 = 0
SMALL_GLR0 = 32
PROJ_COLS = 22 * LANE


def _nt(a, b):
    return lax.dot_general(a, b, (((1,), (1,)), ((), ())), preferred_element_type=F32)


def _tn(a, b):
    return lax.dot_general(a, b, (((0,), (0,)), ((), ())), preferred_element_type=F32)


def _dot(a, b):
    return jnp.dot(a, b, preferred_element_type=F32)


def _split3(x):
    hi = x.astype(BF16)
    r = x - hi.astype(F32)
    mid = r.astype(BF16)
    lo = (r - mid.astype(F32)).astype(BF16)
    return hi, mid, lo


def _layer_norm(y, g, b):
    mu = jnp.mean(y, axis=-1, keepdims=True)
    d = y - mu
    var = jnp.mean(d * d, axis=-1, keepdims=True)
    return d * lax.rsqrt(var + LN_EPS) * g + b


def _rope(x, a, bm, cm):
    return x * a + pltpu.roll(x, LANE - ROPE_DIM // 2, 1) * bm + pltpu.roll(x, ROPE_DIM // 2, 1) * cm


def _proj_kernel(h_ref, w_ref, qa_ref, qb_ref, qc_ref, ka_ref, kb_ref, kc_ref,
                 q_out, cmpk_out, cmpv_out, sel_out, win_out, small_out, gq_out, gk_out, gv_out, og_out, pu_out):
    hb = h_ref[...].astype(BF16)
    plan = ([(q_out, j, "q") for j in range(3)]
            + [(cmpk_out, 0, "k"), (cmpv_out, 0, None), (sel_out, 0, "k"), (sel_out, 1, None),
               (win_out, 0, "k"), (win_out, 1, None), (small_out, 0, None)]
            + [(gq_out, j, None) for j in range(2)] + [(gk_out, j, None) for j in range(2)]
            + [(gv_out, j, None) for j in range(3)] + [(og_out, j, None) for j in range(3)]
            + [(pu_out, j, None) for j in range(2)])
    for c in range(0, len(plan), 2):
        acc = _nt(hb, w_ref[c * LANE:(c + 2) * LANE, :])
        for half in range(2):
            ref, j, kind = plan[c + half]
            x = acc[:, half * LANE:(half + 1) * LANE]
            if kind == "q":
                x = _rope(x, qa_ref[...], qb_ref[...], qc_ref[...])
            elif kind == "k":
                x = _rope(x, ka_ref[...], kb_ref[...], kc_ref[...])
            ref[:, j * LANE:(j + 1) * LANE] = x.astype(ref.dtype)


def _proj(h, w, rope_q, rope_k, seq, tm):
    m = h.shape[0]
    tiles_per_seq = seq // tm
    row = lambda i: (i, 0)
    tab = lambda i: (i % tiles_per_seq, 0)
    widths = [(NSA_WIDTH, BF16), (LANE, F32), (LANE, F32), (256, BF16), (256, BF16), (LANE, F32),
              (GLA_QK_PAD, BF16), (GLA_QK_PAD, BF16), (GLA_WIDTH, BF16), (GLA_WIDTH, BF16), (POOL_WIDTH, BF16)]
    return pl.pallas_call(
        _proj_kernel,
        grid=(m // tm,),
        in_specs=[pl.BlockSpec((tm, D_MODEL), row),
                  pl.BlockSpec((PROJ_COLS, D_MODEL), lambda i: (0, 0), pipeline_mode=pl.Buffered(1))]
                 + [pl.BlockSpec((tm, LANE), tab)] * 6,
        out_specs=[pl.BlockSpec((tm, wd), row) for wd, _ in widths],
        out_shape=[jax.ShapeDtypeStruct((m, wd), dt) for wd, dt in widths],
        compiler_params=pltpu.CompilerParams(dimension_semantics=("parallel",), vmem_limit_bytes=VMEM_LIMIT),
        name="in_proj",
    )(h, w, *rope_q, *rope_k)


def _cmp_kernel(k_ref, v_ref, pos_ref, w1_ref, w2_ref, o_ref):
    nblk = o_ref.shape[0]
    half = CMP_BLOCK // 2
    for kvi, x_ref in enumerate((k_ref, v_ref)):
        lanes = slice(kvi * LANE, (kvi + 1) * LANE)
        ha = jnp.zeros((nblk, LANE), F32)
        hb = jnp.zeros((nblk, LANE), F32)
        for r in range(half):
            t = x_ref[pl.ds(r, nblk, stride=CMP_STRIDE), :]
            ha = ha + _dot((t + pos_ref[kvi, r:r + 1, :]).astype(BF16), w1_ref[kvi, r])
            hb = hb + _dot((t + pos_ref[kvi, half + r:half + r + 1, :]).astype(BF16), w1_ref[kvi, half + r])
        hid = ha + pltpu.roll(hb, nblk - 1, 0)
        hid = jax.nn.gelu(hid, approximate=True)
        o_ref[:, lanes] = _dot(hid.astype(BF16), w2_ref[kvi])


def _compress(cmp_k, cmp_v, pos2, w1bd, w2bd, batch, seq):
    nblk = seq // CMP_STRIDE
    return pl.pallas_call(
        _cmp_kernel,
        grid=(batch,),
        in_specs=[pl.BlockSpec((seq, LANE), lambda b: (b, 0)), pl.BlockSpec((seq, LANE), lambda b: (b, 0)),
                  pl.BlockSpec((2, CMP_BLOCK, LANE), lambda b: (0, 0, 0)),
                  pl.BlockSpec((2, CMP_BLOCK, LANE, LANE), lambda b: (0, 0, 0, 0)),
                  pl.BlockSpec((2, LANE, LANE), lambda b: (0, 0, 0))],
        out_specs=pl.BlockSpec((nblk, 256), lambda b: (b, 0)),
        out_shape=jax.ShapeDtypeStruct((batch * nblk, 256), F32),
        compiler_params=pltpu.CompilerParams(dimension_semantics=("parallel",), vmem_limit_bytes=VMEM_LIMIT),
        name="nsa_compress",
    )(cmp_k, cmp_v, pos2, w1bd, w2bd)


NSA_TQ = 256
NSA_TK = 512
WIN_SPAN = WINDOW + NSA_TQ
LOG2E = 1.4426950408889634


def _nsa_kernel(q_ref, kc_ref, sel_ref, win_ref, oh_ref, g_ref, c2s_ref, eye_ref, perm_ref, o_ref,
                kaug, vsel, vwin, vcmp, qaug, gate_scr, mix_scr):
    tq, tk = NSA_TQ, NSA_TK
    rows3 = NSA_H * tq
    qi = pl.program_id(1)
    n_sel = oh_ref.shape[0] // SEL_BLOCK
    lane2 = lax.broadcasted_iota(jnp.int32, (1, 2 * LANE), 1)
    ones_half = jnp.where(lane2 >= LANE, 1.0, 0.0)

    @pl.when(qi == 0)
    def _():
        kaug[:, 0:LANE] = sel_ref[:, 0:LANE]
        kaug[:, LANE:2 * LANE] = oh_ref[...]
        for g in range(NSA_G):
            vsel[g] = (_dot(sel_ref[:, LANE:2 * LANE], perm_ref[g]) + ones_half).astype(BF16)
            vwin[g] = (_dot(win_ref[:, LANE:2 * LANE], perm_ref[g]) + ones_half).astype(BF16)
            vcmp[g] = _dot(kc_ref[:, LANE:2 * LANE].astype(BF16), perm_ref[g, :, 0:LANE]).astype(BF16)

    q0 = qi * tq
    t3 = q0 + (lax.broadcasted_iota(jnp.int32, (rows3, 1), 0) & (tq - 1))
    t_lane = q0 + lax.broadcasted_iota(jnp.int32, (1, tq), 1)
    gates = jax.nn.sigmoid(g_ref[...])
    for g in range(NSA_G):
        for h in range(NSA_H):
            for br in range(3):
                col = (g * NSA_H + h) * 3 + br
                gate_scr[g, br, h * tq:(h + 1) * tq, :] = jnp.broadcast_to(gates[:, col:col + 1], (tq, LANE))
    kc_b = kc_ref[:, 0:LANE].astype(BF16)
    for g in range(NSA_G):
        glanes = slice(g * HEAD_DIM, (g + 1) * HEAD_DIM)
        qaug[g, :, 0:LANE] = jnp.zeros((rows3, LANE), BF16)
        for h in range(NSA_H):
            c = (NSA_H * g + h) * HEAD_DIM
            qaug[g, h * tq:(h + 1) * tq, glanes] = q_ref[:, c:c + HEAD_DIM]

    ks = pl.multiple_of(jnp.maximum(q0 - WINDOW, 0), tq)
    kw = win_ref[pl.ds(ks, WIN_SPAN), 0:LANE]
    t1 = q0 + lax.broadcasted_iota(jnp.int32, (tq, 1), 0)
    diff = t1 - (ks + lax.broadcasted_iota(jnp.int32, (1, WIN_SPAN), 1))
    wbias = jnp.where((diff >= 0) & (diff < WINDOW), 0.0, NEG)
    wbias = jnp.concatenate([wbias] * NSA_H, axis=0)
    sws = [_nt(qaug[g, :, 0:LANE], kw) + wbias for g in range(NSA_G)]
    scs = [_nt(qaug[g, :, 0:LANE], kc_b) for g in range(NSA_G)]
    pws = [jnp.exp2(sw - jnp.max(sw, axis=1, keepdims=True)).astype(BF16) for sw in sws]
    a_wins = [_dot(pws[g], vwin[g, pl.ds(ks, WIN_SPAN), :]) for g in range(NSA_G)]
    o_win = [a[:, 0:LANE] * (gate_scr[g, 2] / a[:, LANE:2 * LANE]) for g, a in enumerate(a_wins)]

    for g in range(NSA_G):
        s = scs[g]
        ncol = lax.broadcasted_iota(jnp.int32, (1, s.shape[1]), 1)
        cmask = (CMP_STRIDE * ncol + (CMP_BLOCK - 1)) <= t3
        sm = jnp.where(cmask, s, NEG)
        m = jnp.max(sm, axis=1, keepdims=True)
        e = jnp.where(cmask, jnp.exp2(sm - m), 0.0)
        p = e / jnp.maximum(jnp.sum(e, axis=1, keepdims=True), 1e-30)
        o_cmp = _dot(p.astype(BF16), vcmp[g])
        mix_scr[g] = gate_scr[g, 0] * o_cmp + o_win[g]

        psum = p[0:tq] + p[tq:2 * tq] + p[2 * tq:3 * tq]
        c2s = c2s_ref[...]
        imp = sum(_nt(c2s, term) for term in _split3(psum))
        jb = lax.broadcasted_iota(jnp.int32, (n_sel, 1), 0)
        cur = t_lane // SEL_BLOCK
        imp = jnp.where((jb == 0) | (jb == cur) | (jb == cur - 1), FORCE_SCORE, imp)
        imp = jnp.where(jb > cur, -FORCE_SCORE, imp)
        rank = jnp.zeros(imp.shape, F32)
        for i in range(n_sel):
            ri = imp[i:i + 1, :]
            ahead = (ri > imp) | ((ri == imp) & (jb > i))
            rank = rank + jnp.where(ahead, 1.0, 0.0)
        sel_t = jnp.where(rank < float(min(SEL_TOP_N, n_sel)), 1.0, 0.0).astype(BF16)
        sel_t = jnp.concatenate([sel_t, jnp.zeros((LANE - n_sel, tq), BF16)], axis=0)
        sel = _nt(eye_ref[...], sel_t)
        bias = ((sel - 1.0) * (-SEL_NEG)).astype(BF16)
        for h in range(NSA_H):
            qaug[g, h * tq:(h + 1) * tq, LANE:2 * LANE] = bias

    last = (q0 + tq - 1) // tk

    def selected(n_chunks):
        lo = (n_chunks - 1) * tk
        kpos = lo + lax.broadcasted_iota(jnp.int32, (1, tk), 1)
        cbias = jnp.concatenate([jnp.where(kpos <= t1, 0.0, NEG)] * NSA_H, axis=0)
        heads = []
        sbs = [_nt(qaug[g], kaug[lo:lo + tk, :]) + cbias for g in range(NSA_G)]
        sas = [_nt(qaug[g], kaug[0:lo, :]) for g in range(NSA_G)] if lo else None
        for g in range(NSA_G):
            sb = sbs[g]
            blocks = [sb[:, j * LANE:(j + 1) * LANE] for j in range(tk // LANE)]
            if lo:
                sa = sas[g]
                blocks += [sa[:, j * LANE:(j + 1) * LANE] for j in range(lo // LANE)]
            m = jnp.max(functools.reduce(jnp.maximum, blocks), axis=1, keepdims=True)
            acc = _dot(jnp.exp2(sb - m).astype(BF16), vsel[g, lo:lo + tk, :])
            if lo:
                acc = acc + _dot(jnp.exp2(sa - m).astype(BF16), vsel[g, 0:lo, :])
            mix = mix_scr[g] + acc[:, 0:LANE] * (gate_scr[g, 1] / acc[:, LANE:2 * LANE])
            heads += [mix[h * tq:(h + 1) * tq] for h in range(NSA_H)]
        low = lax.broadcasted_iota(jnp.int32, (1, LANE), 1) < HEAD_DIM
        for j in range(len(heads) // 2):
            pair = jnp.where(low, heads[2 * j], heads[2 * j + 1])
            o_ref[:, j * LANE:(j + 1) * LANE] = pair.astype(o_ref.dtype)

    for n_chunks in range(1, kaug.shape[0] // tk + 1):
        pl.when(last == n_chunks - 1)(functools.partial(selected, n_chunks))


def _nsa(q, kc, sel, win, small, onehot, c2s_t, eye, perm, batch, seq):
    nq = seq // NSA_TQ
    nblk = seq // CMP_STRIDE
    rows3 = NSA_H * NSA_TQ
    qrow = lambda b, i: (b * nq + i, 0)
    per_b = lambda b, i: (b, 0)
    const = lambda b, i: (0, 0)
    return pl.pallas_call(
        _nsa_kernel,
        grid=(batch, nq),
        in_specs=[pl.BlockSpec((NSA_TQ, NSA_WIDTH), qrow),
                  pl.BlockSpec((nblk, 256), per_b),
                  pl.BlockSpec((seq, 256), per_b),
                  pl.BlockSpec((seq, 256), per_b),
                  pl.BlockSpec((seq, LANE), const),
                  pl.BlockSpec((NSA_TQ, LANE), qrow),
                  pl.BlockSpec(c2s_t.shape, const),
                  pl.BlockSpec((NSA_TQ, NSA_TQ), const),
                  pl.BlockSpec(perm.shape, lambda b, i: (0, 0, 0))],
        out_specs=pl.BlockSpec((NSA_TQ, NSA_WIDTH), qrow),
        out_shape=jax.ShapeDtypeStruct((batch * seq, NSA_WIDTH), BF16),
        scratch_shapes=[pltpu.VMEM((seq, 2 * LANE), BF16),
                        pltpu.VMEM((NSA_G, seq, 2 * LANE), BF16),
                        pltpu.VMEM((NSA_G, seq, 2 * LANE), BF16),
                        pltpu.VMEM((NSA_G, nblk, LANE), BF16),
                        pltpu.VMEM((NSA_G, rows3, 2 * LANE), BF16),
                        pltpu.VMEM((NSA_G, 3, rows3, LANE), F32),
                        pltpu.VMEM((NSA_G, rows3, LANE), F32)],
        compiler_params=pltpu.CompilerParams(dimension_semantics=("parallel", "arbitrary"),
                                             vmem_limit_bytes=VMEM_LIMIT),
        name="nsa_attention",
    )(q, kc, sel, win, onehot, small, c2s_t, eye, perm)


GLA_TILE = 256


def _log_sigmoid(x):
    return jnp.minimum(x, 0.0) - jnp.log(1.0 + jnp.exp(-jnp.abs(x)))


def _gla_kernel(q_ref, k_ref, v_ref, og_ref, small_ref, wg_ref, bg_ref, ng_ref, ltri_ref, lsum_ref, bd_ref,
                o_ref, qd_s, ke_s, dec_s, oacc_s, st_s):
    seq = q_ref.shape[0]
    ch = GLA_CHUNK
    nh = GLA_HEADS
    tile = GLA_TILE
    qk_head = lax.broadcasted_iota(jnp.int32, (1, GLA_QK_PAD), 1) // GLA_DK
    low = lax.broadcasted_iota(jnp.int32, (1, LANE), 1) < GLA_DV
    ri = lax.broadcasted_iota(jnp.int32, (tile, 1), 0)
    cj = lax.broadcasted_iota(jnp.int32, (1, tile), 1)
    intra = (cj <= ri) & ((cj // ch) == (ri // ch))
    intra = jnp.concatenate([intra] * nh, axis=1)

    def pre(i, _):
        for half in range(2):
            pre_tile(2 * i + half)
        return 0

    def pre_tile(i):
        rs = pl.ds(pl.multiple_of(i * tile, tile), tile)
        x_hi, x_mid, _ = _split3(small_ref[rs, :])
        w_hi, w_mid, _ = _split3(wg_ref[...])
        z = _dot(x_hi, w_hi) + (_dot(x_hi, w_mid) + _dot(x_mid, w_hi)) + bg_ref[...]
        gl = _log_sigmoid(z) * (1.0 / GLA_GATE_NORM)
        g_hi, g_mid, _ = _split3(gl)
        bc = _dot(ltri_ref[...], g_hi) + _dot(ltri_ref[...], g_mid)
        bl = _dot(lsum_ref[...], g_hi) + _dot(lsum_ref[...], g_mid)
        dec = jnp.exp(bl)
        qd = (q_ref[rs, :].astype(F32) * (GLA_DK ** -0.5) * jnp.exp(bc)).astype(BF16)
        k_inv = k_ref[rs, :].astype(F32) * jnp.exp(-bc)
        ki = k_inv.astype(BF16)
        qd_s[rs, :] = qd
        ke_s[rs, :] = (k_inv * dec).astype(BF16)
        dec_s[rs, :] = dec
        k_heads = jnp.concatenate([jnp.where(qk_head == h, ki, jnp.zeros_like(ki)) for h in range(nh)], axis=0)
        a = jnp.where(intra, _nt(qd, k_heads), 0.0).astype(BF16)
        for pr in range(nh // 2):
            lanes = slice(pr * LANE, (pr + 1) * LANE)
            vp = v_ref[rs, lanes]
            zero = jnp.zeros_like(vp)
            v_pair = jnp.concatenate([jnp.where(low, vp, zero), jnp.where(low, zero, vp)], axis=0)
            oacc_s[rs, lanes] = _dot(a[:, 2 * pr * tile:(2 * pr + 2) * tile], v_pair)

    lax.fori_loop(0, seq // (2 * tile), pre, 0)

    st_mask = (lax.broadcasted_iota(jnp.int32, (GLA_WIDTH, 1), 0) // GLA_DV) == qk_head
    st_s[...] = jnp.zeros(st_s.shape, F32)
    per_iter = tile // ch

    def scan(i, _):
        st = st_s[...]
        base = pl.multiple_of(i * tile, tile)
        dts = [_tn(v_ref[pl.ds(base + c * ch, ch), :], ke_s[pl.ds(base + c * ch, ch), :]) for c in range(per_iter)]
        for c in range(per_iter):
            rs = pl.ds(base + c * ch, ch)
            oacc_s[rs, :] = oacc_s[rs, :] + _nt(qd_s[rs, :], st.astype(BF16))
            st = st * dec_s[pl.ds(base + c * ch, 1), :] + jnp.where(st_mask, dts[c], 0.0)
        st_s[...] = st
        return 0

    lax.fori_loop(0, seq // tile, scan, 0)

    def post(i, _):
        tiles = [pl.ds(pl.multiple_of((2 * i + half) * tile, tile), tile) for half in range(2)]
        os_ = [oacc_s[rs, :] for rs in tiles]
        mss = [_dot((o * o).astype(BF16), bd_ref[...]) for o in os_]
        for rs, o, ms in zip(tiles, os_, mss):
            og = og_ref[rs, :].astype(F32)
            y = o * lax.rsqrt(ms + RMS_EPS) * ng_ref[...] * (og * jax.nn.sigmoid(og))
            o_ref[rs, :] = y.astype(o_ref.dtype)
        return 0

    lax.fori_loop(0, seq // (2 * tile), post, 0)


def _gla(gq, gk, gv, og, small, wg, bg, ng, ltri, lsum, bd, batch, seq):
    per_b = lambda b: (b, 0)
    const = lambda b: (0, 0)
    return pl.pallas_call(
        _gla_kernel,
        grid=(batch,),
        in_specs=[pl.BlockSpec((seq, GLA_QK_PAD), per_b), pl.BlockSpec((seq, GLA_QK_PAD), per_b),
                  pl.BlockSpec((seq, GLA_WIDTH), per_b), pl.BlockSpec((seq, GLA_WIDTH), per_b),
                  pl.BlockSpec((seq, LANE), per_b),
                  pl.BlockSpec(wg.shape, const), pl.BlockSpec(bg.shape, const), pl.BlockSpec(ng.shape, const),
                  pl.BlockSpec(ltri.shape, const), pl.BlockSpec(lsum.shape, const), pl.BlockSpec(bd.shape, const)],
        out_specs=pl.BlockSpec((seq, GLA_WIDTH), per_b),
        out_shape=jax.ShapeDtypeStruct((batch * seq, GLA_WIDTH), BF16),
        scratch_shapes=[pltpu.VMEM((seq, GLA_QK_PAD), BF16), pltpu.VMEM((seq, GLA_QK_PAD), BF16),
                        pltpu.VMEM((seq, GLA_QK_PAD), F32),
                        pltpu.VMEM((seq, GLA_WIDTH), F32), pltpu.VMEM((GLA_WIDTH, GLA_QK_PAD), F32)],
        compiler_params=pltpu.CompilerParams(dimension_semantics=("parallel",), vmem_limit_bytes=VMEM_LIMIT),
        name="gla",
    )(gq, gk, gv, og, small, wg, bg, ng, ltri, lsum, bd)


POOL_TILE = 256
POOL_HALO = 16


def _pool_kernel(u_ref, w_ref, scale_ref, o_ref):
    seq = u_ref.shape[0]
    lane = lax.broadcasted_iota(jnp.int32, (1, POOL_WIDTH), 1)
    grp = lane // POOL_GROUP
    win = jnp.where(grp == 0, POOL_WINDOWS[0],
                    jnp.where(grp == 1, POOL_WINDOWS[1], jnp.where(grp == 2, POOL_WINDOWS[2], POOL_WINDOWS[3])))
    for i in range(seq // POOL_TILE):
        r0 = i * POOL_TILE
        u = u_ref[r0:r0 + POOL_TILE, :].astype(F32)
        if i > 0:
            x = u_ref[r0 - POOL_HALO:r0 + POOL_TILE, :].astype(F32)
        else:
            x = jnp.concatenate([jnp.zeros((POOL_HALO, POOL_WIDTH), F32), u], axis=0)
        tot = x
        span = 1
        for gi, w in enumerate(POOL_WINDOWS):
            while span < w:
                x = x + pltpu.roll(x, span, 0)
                span *= 2
            tot = jnp.where(grp == gi, x, tot)
        tot = tot[POOL_HALO:, :]
        t = r0 + lax.broadcasted_iota(jnp.int32, (POOL_TILE, 1), 0)
        cnt = jnp.minimum(t + 1, win).astype(F32)
        pooled = tot / cnt - u
        o_ref[r0:r0 + POOL_TILE, :] = (_dot(pooled.astype(BF16), w_ref[...]) * scale_ref[...]).astype(o_ref.dtype)


def _pool(u, wbd, scale, batch, seq):
    per_b = lambda b: (b, 0)
    const = lambda b: (0, 0)
    return pl.pallas_call(
        _pool_kernel,
        grid=(batch,),
        in_specs=[pl.BlockSpec((seq, POOL_WIDTH), per_b),
                  pl.BlockSpec(wbd.shape, const), pl.BlockSpec(scale.shape, const)],
        out_specs=pl.BlockSpec((seq, POOL_WIDTH), per_b),
        out_shape=jax.ShapeDtypeStruct((batch * seq, POOL_WIDTH), BF16),
        compiler_params=pltpu.CompilerParams(dimension_semantics=("parallel",), vmem_limit_bytes=VMEM_LIMIT),
        name="pool",
    )(u, wbd, scale)


FFN_CHUNK = 1024


def _mix_ffn_kernel(nsa_ref, gla_ref, pool_ref, h_ref, wo_ref, g1_ref, b1_ref, wu_ref, wd_ref, g2_ref, b2_ref,
                    o_ref):
    half = h_ref.shape[0] // 2
    parts = [slice(0, half), slice(half, 2 * half)]
    mixed = []
    for rs in parts:
        acc = _dot(nsa_ref[rs, :], wo_ref[0:NSA_WIDTH, :])
        acc = acc + _dot(gla_ref[rs, :], wo_ref[NSA_WIDTH:NSA_WIDTH + GLA_WIDTH, :])
        mixed.append(acc + _dot(pool_ref[rs, :], wo_ref[NSA_WIDTH + GLA_WIDTH:, :]))
    hs = [_layer_norm(ALPHA * h_ref[rs, :] + a, g1_ref[...], b1_ref[...]) for rs, a in zip(parts, mixed)]
    hbs = [h.astype(BF16) for h in hs]
    accs = [ALPHA * h for h in hs]
    for c in range(D_FF // FFN_CHUNK):
        cs = slice(c * FFN_CHUNK, (c + 1) * FFN_CHUNK)
        us = [jnp.maximum(_dot(hb, wu_ref[:, cs]), 0.0) for hb in hbs]
        accs = [acc + _dot((u * u).astype(BF16), wd_ref[cs, :]) for acc, u in zip(accs, us)]
    for rs, acc in zip(parts, accs):
        o_ref[rs, :] = _layer_norm(acc, g2_ref[...], b2_ref[...])


def _mix_ffn(nsa, gla, pool, h, wo, g1, b1, wu, wd, g2, b2, layer, tm):
    m = h.shape[0]
    row = lambda i: (i, 0)
    const = lambda i: (0, 0)
    vec = pl.BlockSpec((1, D_MODEL), const)
    resident = lambda a: pl.BlockSpec((None,) + a.shape[1:], lambda i: (layer, 0, 0), pipeline_mode=pl.Buffered(1))
    return pl.pallas_call(
        _mix_ffn_kernel,
        grid=(m // tm,),
        in_specs=[pl.BlockSpec((tm, NSA_WIDTH), row), pl.BlockSpec((tm, GLA_WIDTH), row),
                  pl.BlockSpec((tm, POOL_WIDTH), row), pl.BlockSpec((tm, D_MODEL), row),
                  resident(wo), vec, vec, resident(wu), resident(wd), vec, vec],
        out_specs=pl.BlockSpec((tm, D_MODEL), row),
        out_shape=jax.ShapeDtypeStruct((m, D_MODEL), F32),
        compiler_params=pltpu.CompilerParams(dimension_semantics=("parallel",), vmem_limit_bytes=FFN_VMEM_LIMIT),
        name="out_proj_ffn",
    )(nsa, gla, pool, h, wo, g1, b1, wu, wd, g2, b2)


def _rope_tables(seq, scale):
    half = ROPE_DIM // 2
    inv = ROPE_THETA ** (-jnp.arange(half, dtype=F32) * 2.0 / ROPE_DIM)
    ang = jnp.arange(seq).astype(F32)[:, None] * inv[None, :]
    cos, sin = jnp.cos(ang), jnp.sin(ang)
    ones = jnp.ones((seq, HEAD_DIM - ROPE_DIM), F32)
    zeros = jnp.zeros((seq, HEAD_DIM - ROPE_DIM), F32)
    zh = jnp.zeros((seq, half), F32)
    a = jnp.concatenate([cos, cos, ones], axis=1)
    bm = jnp.concatenate([-sin, zh, zeros], axis=1)
    cm = jnp.concatenate([zh, sin, zeros], axis=1)
    reps = LANE // HEAD_DIM
    return tuple(jnp.tile(t, (1, reps)) * scale for t in (a, bm, cm))


def _static_tables(seq):
    n_sel = seq // SEL_BLOCK
    nblk = seq // CMP_STRIDE
    n_cmp = (seq - CMP_BLOCK) // CMP_STRIDE + 1
    onehot = np.zeros((seq, LANE), np.float32)
    onehot[np.arange(seq), np.arange(seq) // SEL_BLOCK] = 1.0
    c_lo = CMP_STRIDE * np.arange(n_cmp)[:, None]
    s_lo = SEL_BLOCK * np.arange(n_sel)[None, :]
    overlap = np.clip(np.minimum(c_lo + CMP_BLOCK, s_lo + SEL_BLOCK) - np.maximum(c_lo, s_lo), 0, None)
    c2s_t = np.zeros((n_sel, nblk), np.float32)
    c2s_t[:, :n_cmp] = (overlap.astype(np.float32) / CMP_BLOCK).T
    eye = np.eye(NSA_TQ, dtype=np.float32)
    perm = np.zeros((NSA_G, LANE, 2 * LANE), np.float32)
    for g in range(NSA_G):
        for rep in range(2):
            perm[g, g * HEAD_DIM + np.arange(HEAD_DIM), rep * HEAD_DIM + np.arange(HEAD_DIM)] = 1.0
    r = np.arange(GLA_TILE)
    same = (r[:, None] // GLA_CHUNK) == (r[None, :] // GLA_CHUNK)
    ltri = (same & (r[None, :] <= r[:, None])).astype(np.float32)
    lsum = same.astype(np.float32)
    hv = np.arange(GLA_WIDTH) // GLA_DV
    bd = (hv[:, None] == hv[None, :]).astype(np.float32) / GLA_DV
    as_bf = lambda a: jnp.asarray(a, BF16)
    return dict(onehot=as_bf(onehot), c2s_t=as_bf(c2s_t), eye=as_bf(eye), perm=as_bf(perm), ltri=as_bf(ltri),
                lsum=as_bf(lsum), bd=as_bf(bd))


def _block_diag2(w):
    z = jnp.zeros_like(w)
    return jnp.concatenate([jnp.concatenate([w, z], axis=-1), jnp.concatenate([z, w], axis=-1)], axis=-2)


def _layer_weights(w_in, cmp_pos, cmp_w1, cmp_w2, w_gate2, b_gate, norm_g, pool_w, pool_scale):
    o_nq, o_nkv, o_gate = 0, NSA_WIDTH, NSA_WIDTH + 768
    o_gq = o_gate + 18
    o_gk = o_gq + 192
    o_gv = o_gk + 192
    o_glr = o_gv + GLA_WIDTH
    o_og = o_glr + GLA_RANK
    o_pu = o_og + GLA_WIDTH
    wt = w_in.T
    zr = lambda n: jnp.zeros((n, D_MODEL), F32)
    small = jnp.concatenate([wt[o_gate:o_gq], zr(SMALL_GLR0 - 18), wt[o_glr:o_og],
                             zr(LANE - SMALL_GLR0 - GLA_RANK)], axis=0)
    wp = jnp.concatenate([wt[o_nq:o_gate], small,
                          wt[o_gq:o_gk], zr(GLA_QK_PAD - 192), wt[o_gk:o_gv], zr(GLA_QK_PAD - 192),
                          wt[o_gv:o_glr], wt[o_og:o_pu], wt[o_pu:]], axis=0).astype(BF16)
    pos2 = jnp.concatenate([cmp_pos, cmp_pos], axis=-1)
    w1bd = _block_diag2(cmp_w1.reshape(2, CMP_BLOCK, HEAD_DIM, cmp_w1.shape[-1])).astype(BF16)
    w2bd = _block_diag2(cmp_w2).astype(BF16)
    wg = jnp.zeros((LANE, GLA_QK_PAD), F32).at[SMALL_GLR0:SMALL_GLR0 + GLA_RANK, :192].set(w_gate2)
    bg = jnp.zeros((1, GLA_QK_PAD), F32).at[0, :192].set(b_gate)
    ng = norm_g.reshape(1, GLA_WIDTH)
    pw = jnp.zeros((POOL_WIDTH, POOL_WIDTH), F32)
    for gi in range(len(POOL_WINDOWS)):
        sl = slice(gi * POOL_GROUP, (gi + 1) * POOL_GROUP)
        pw = pw.at[sl, sl].set(pool_w[gi])
    return dict(wp=wp, pos2=pos2, w1bd=w1bd, w2bd=w2bd, wg=wg, bg=bg, ng=ng, pw=pw.astype(BF16),
                ps=pool_scale.reshape(1, POOL_WIDTH))


@jax.jit
def kernel(x, w_in, cmp_pos, cmp_w1, cmp_w2, gla_w_gate2, gla_b_gate, gla_norm_g, pool_w, pool_scale,
           w_out, ln1_g, ln1_b, w_up, w_down, ln2_g, ln2_b):
    batch, seq, d = x.shape
    assert d == D_MODEL and seq % (2 * GLA_TILE) == 0 and seq >= WIN_SPAN and seq // SEL_BLOCK <= LANE
    tm = 512
    st = _static_tables(seq)
    rope_q = _rope_tables(seq, HEAD_DIM ** -0.5 * LOG2E)
    rope_k = _rope_tables(seq, 1.0)
    wo_all, wu_all, wd_all = w_out.astype(BF16), w_up.astype(BF16), w_down.astype(BF16)
    h = x.reshape(batch * seq, d)
    for l in range(w_in.shape[0]):
        lw = _layer_weights(w_in[l], cmp_pos[l], cmp_w1[l], cmp_w2[l], gla_w_gate2[l], gla_b_gate[l],
                            gla_norm_g[l], pool_w[l], pool_scale[l])
        q, cmp_k, cmp_v, sel, win, small, gq, gk, gv, og, pu = _proj(h, lw["wp"], rope_q, rope_k, seq, tm)
        kc = _compress(cmp_k, cmp_v, lw["pos2"], lw["w1bd"], lw["w2bd"], batch, seq)
        nsa = _nsa(q, kc, sel, win, small, st["onehot"], st["c2s_t"], st["eye"], st["perm"], batch, seq)
        gla = _gla(gq, gk, gv, og, small, lw["wg"], lw["bg"], lw["ng"], st["ltri"], st["lsum"], st["bd"],
                   batch, seq)
        pool = _pool(pu, lw["pw"], lw["ps"], batch, seq)
        h = _mix_ffn(nsa, gla, pool, h, wo_all, ln1_g[l].reshape(1, d), ln1_b[l].reshape(1, d),
                     wu_all, wd_all, ln2_g[l].reshape(1, d), ln2_b[l].reshape(1, d), l, 2 * tm)
    return h.reshape(batch, seq, d)
```

```python
import functools
import numpy as np
import jax
import jax.numpy as jnp
from jax import lax
from jax.experimental import pallas as pl
from jax.experimental.pallas import tpu as pltpu

F32 = jnp.float32
BF16 = jnp.bfloat16

D_MODEL = 1024
HEAD_DIM = 64
NSA_G = 2
NSA_H = 3
NSA_WIDTH = NSA_G * NSA_H * HEAD_DIM
CMP_BLOCK = 32
CMP_STRIDE = 16
SEL_BLOCK = 64
SEL_TOP_N = 16
WINDOW = 512
FORCE_SCORE = 1.0e4
GLA_HEADS = 6
GLA_DK = 32
GLA_DV = 64
GLA_WIDTH = GLA_HEADS * GLA_DV
GLA_RANK = 16
GLA_GATE_NORM = 16.0
GLA_CHUNK = 64
GLA_QK_PAD = 256
POOL_WINDOWS = (2, 4, 8, 16)
POOL_GROUP = 64
POOL_WIDTH = 256
D_FF = 4 * D_MODEL
ROPE_THETA = 500000.0
ROPE_DIM = HEAD_DIM // 4
LN_EPS = 1e-5
RMS_EPS = 1e-6
DEPTH = 4
ALPHA = (2 * DEPTH) ** 0.25

LANE = 128
NEG = -1e30
SEL_NEG = -30000.0

VMEM_LIMIT = 48 * 1024 * 1024
FFN_VMEM_LIMIT = 54 * 1024 * 1024

SMALL_GATE0 = 0
SMALL_GLR0 = 32
PROJ_COLS = 22 * LANE


def _nt(a, b):
    return lax.dot_general(a, b, (((1,), (1,)), ((), ())), preferred_element_type=F32)


def _tn(a, b):
    return lax.dot_general(a, b, (((0,), (0,)), ((), ())), preferred_element_type=F32)


def _dot(a, b):
    return jnp.dot(a, b, preferred_element_type=F32)


def _split3(x):
    hi = x.astype(BF16)
    r = x - hi.astype(F32)
    mid = r.astype(BF16)
    lo = (r - mid.astype(F32)).astype(BF16)
    return hi, mid, lo


def _layer_norm(y, g, b):
    mu = jnp.mean(y, axis=-1, keepdims=True)
    d = y - mu
    var = jnp.mean(d * d, axis=-1, keepdims=True)
    return d * lax.rsqrt(var + LN_EPS) * g + b


def _rope(x, a, bm, cm):
    return x * a + pltpu.roll(x, LANE - ROPE_DIM // 2, 1) * bm + pltpu.roll(x, ROPE_DIM // 2, 1) * cm


def _proj_kernel(h_ref, w_ref, qa_ref, qb_ref, qc_ref, ka_ref, kb_ref, kc_ref,
                 q_out, cmpk_out, cmpv_out, sel_out, win_out, small_out, gq_out, gk_out, gv_out, og_out, pu_out):
    hb = h_ref[...].astype(BF16)
    plan = ([(q_out, j, "q") for j in range(3)]
            + [(cmpk_out, 0, "k"), (cmpv_out, 0, None), (sel_out, 0, "k"), (sel_out, 1, None),
               (win_out, 0, "k"), (win_out, 1, None), (small_out, 0, None)]
            + [(gq_out, j, None) for j in range(2)] + [(gk_out, j, None) for j in range(2)]
            + [(gv_out, j, None) for j in range(3)] + [(og_out, j, None) for j in range(3)]
            + [(pu_out, j, None) for j in range(2)])
    for c in range(0, len(plan), 2):
        acc = _nt(hb, w_ref[c * LANE:(c + 2) * LANE, :])
        for half in range(2):
            ref, j, kind = plan[c + half]
            x = acc[:, half * LANE:(half + 1) * LANE]
            if kind == "q":
                x = _rope(x, qa_ref[...], qb_ref[...], qc_ref[...])
            elif kind == "k":
                x = _rope(x, ka_ref[...], kb_ref[...], kc_ref[...])
            ref[:, j * LANE:(j + 1) * LANE] = x.astype(ref.dtype)


def _proj(h, w, rope_q, rope_k, seq, tm):
    m = h.shape[0]
    tiles_per_seq = seq // tm
    row = lambda i: (i, 0)
    tab = lambda i: (i % tiles_per_seq, 0)
    widths = [(NSA_WIDTH, BF16), (LANE, F32), (LANE, F32), (256, BF16), (256, BF16), (LANE, F32),
              (GLA_QK_PAD, BF16), (GLA_QK_PAD, BF16), (GLA_WIDTH, BF16), (GLA_WIDTH, BF16), (POOL_WIDTH, BF16)]
    return pl.pallas_call(
        _proj_kernel,
        grid=(m // tm,),
        in_specs=[pl.BlockSpec((tm, D_MODEL), row),
                  pl.BlockSpec((PROJ_COLS, D_MODEL), lambda i: (0, 0), pipeline_mode=pl.Buffered(1))]
                 + [pl.BlockSpec((tm, LANE), tab)] * 6,
        out_specs=[pl.BlockSpec((tm, wd), row) for wd, _ in widths],
        out_shape=[jax.ShapeDtypeStruct((m, wd), dt) for wd, dt in widths],
        compiler_params=pltpu.CompilerParams(dimension_semantics=("parallel",), vmem_limit_bytes=VMEM_LIMIT),
        name="in_proj",
    )(h, w, *rope_q, *rope_k)


def _permute_rows_kernel(segments, w_ref, o_ref, stage):
    stage[...] = jnp.zeros(stage.shape, F32)
    for dst, src, n in segments:
        stage[dst:dst + n, :] = w_ref[src:src + n, :]
    o_ref[...] = stage[...].astype(o_ref.dtype)


def _permute_rows(wt, segments):
    return pl.pallas_call(
        functools.partial(_permute_rows_kernel, segments),
        out_shape=jax.ShapeDtypeStruct((PROJ_COLS, D_MODEL), BF16),
        scratch_shapes=[pltpu.VMEM((PROJ_COLS, D_MODEL), F32)],
        compiler_params=pltpu.CompilerParams(vmem_limit_bytes=VMEM_LIMIT),
        name="in_proj_weights",
    )(wt)


def _cmp_kernel(k_ref, v_ref, pos_ref, w1_ref, w2_ref, o_ref):
    nblk = o_ref.shape[0]
    half = CMP_BLOCK // 2
    for kvi, x_ref in enumerate((k_ref, v_ref)):
        lanes = slice(kvi * LANE, (kvi + 1) * LANE)
        ha = jnp.zeros((nblk, LANE), F32)
        hb = jnp.zeros((nblk, LANE), F32)
        for r in range(half):
            t = x_ref[pl.ds(r, nblk, stride=CMP_STRIDE), :]
            ha = ha + _dot((t + pos_ref[kvi, r:r + 1, :]).astype(BF16), w1_ref[kvi, r])
            hb = hb + _dot((t + pos_ref[kvi, half + r:half + r + 1, :]).astype(BF16), w1_ref[kvi, half + r])
        hid = ha + pltpu.roll(hb, nblk - 1, 0)
        hid = jax.nn.gelu(hid, approximate=True)
        o_ref[:, lanes] = _dot(hid.astype(BF16), w2_ref[kvi])


def _compress(cmp_k, cmp_v, pos2, w1bd, w2bd, batch, seq):
    nblk = seq // CMP_STRIDE
    return pl.pallas_call(
        _cmp_kernel,
        grid=(batch,),
        in_specs=[pl.BlockSpec((seq, LANE), lambda b: (b, 0)), pl.BlockSpec((seq, LANE), lambda b: (b, 0)),
                  pl.BlockSpec((2, CMP_BLOCK, LANE), lambda b: (0, 0, 0)),
                  pl.BlockSpec((2, CMP_BLOCK, LANE, LANE), lambda b: (0, 0, 0, 0)),
                  pl.BlockSpec((2, LANE, LANE), lambda b: (0, 0, 0))],
        out_specs=pl.BlockSpec((nblk, 256), lambda b: (b, 0)),
        out_shape=jax.ShapeDtypeStruct((batch * nblk, 256), F32),
        compiler_params=pltpu.CompilerParams(dimension_semantics=("parallel",), vmem_limit_bytes=VMEM_LIMIT),
        name="nsa_compress",
    )(cmp_k, cmp_v, pos2, w1bd, w2bd)


NSA_TQ = 256
NSA_TK = 256
WIN_SPAN = WINDOW + NSA_TQ
LOG2E = 1.4426950408889634


def _nsa_kernel(q_ref, kc_ref, sel_ref, win_ref, oh_ref, g_ref, c2s_ref, eye_ref, perm_ref, o_ref,
                kaug, vsel, vwin, vcmp, qaug, gate_scr, mix_scr):
    tq, tk = NSA_TQ, NSA_TK
    rows3 = NSA_H * tq
    qi = pl.program_id(1)
    n_sel = oh_ref.shape[0] // SEL_BLOCK
    lane2 = lax.broadcasted_iota(jnp.int32, (1, 2 * LANE), 1)
    ones_half = jnp.where(lane2 >= LANE, 1.0, 0.0)

    @pl.when(qi == 0)
    def _():
        kaug[:, 0:LANE] = sel_ref[:, 0:LANE]
        kaug[:, LANE:2 * LANE] = oh_ref[...]
        for g in range(NSA_G):
            vsel[g] = (_dot(sel_ref[:, LANE:2 * LANE], perm_ref[g]) + ones_half).astype(BF16)
            vwin[g] = (_dot(win_ref[:, LANE:2 * LANE], perm_ref[g]) + ones_half).astype(BF16)
            vcmp[g] = _dot(kc_ref[:, LANE:2 * LANE].astype(BF16), perm_ref[g, :, 0:LANE]).astype(BF16)

    q0 = qi * tq
    t3 = q0 + (lax.broadcasted_iota(jnp.int32, (rows3, 1), 0) & (tq - 1))
    t_lane = q0 + lax.broadcasted_iota(jnp.int32, (1, tq), 1)
    gates = jax.nn.sigmoid(g_ref[...])
    for g in range(NSA_G):
        for h in range(NSA_H):
            for br in range(3):
                col = (g * NSA_H + h) * 3 + br
                gate_scr[g, br, h * tq:(h + 1) * tq, :] = jnp.broadcast_to(gates[:, col:col + 1], (tq, LANE))
    kc_b = kc_ref[:, 0:LANE].astype(BF16)
    for g in range(NSA_G):
        glanes = slice(g * HEAD_DIM, (g + 1) * HEAD_DIM)
        qaug[g, :, 0:LANE] = jnp.zeros((rows3, LANE), BF16)
        for h in range(NSA_H):
            c = (NSA_H * g + h) * HEAD_DIM
            qaug[g, h * tq:(h + 1) * tq, glanes] = q_ref[:, c:c + HEAD_DIM]

    ks = pl.multiple_of(jnp.maximum(q0 - WINDOW, 0), tq)
    kw = win_ref[pl.ds(ks, WIN_SPAN), 0:LANE]
    t1 = q0 + lax.broadcasted_iota(jnp.int32, (tq, 1), 0)
    diff = t1 - (ks + lax.broadcasted_iota(jnp.int32, (1, WIN_SPAN), 1))
    wbias = jnp.where((diff >= 0) & (diff < WINDOW), 0.0, NEG)
    wbias = jnp.concatenate([wbias] * NSA_H, axis=0)
    sws = [_nt(qaug[g, :, 0:LANE], kw) + wbias for g in range(NSA_G)]
    scs = [_nt(qaug[g, :, 0:LANE], kc_b) for g in range(NSA_G)]
    pws = [jnp.exp2(sw - jnp.max(sw, axis=1, keepdims=True)).astype(BF16) for sw in sws]
    a_wins = [_dot(pws[g], vwin[g, pl.ds(ks, WIN_SPAN), :]) for g in range(NSA_G)]
    o_win = [a[:, 0:LANE] * (gate_scr[g, 2] / a[:, LANE:2 * LANE]) for g, a in enumerate(a_wins)]

    for g in range(NSA_G):
        s = scs[g]
        ncol = lax.broadcasted_iota(jnp.int32, (1, s.shape[1]), 1)
        cmask = (CMP_STRIDE * ncol + (CMP_BLOCK - 1)) <= t3
        sm = jnp.where(cmask, s, NEG)
        m = jnp.max(sm, axis=1, keepdims=True)
        e = jnp.where(cmask, jnp.exp2(sm - m), 0.0)
        p = e / jnp.maximum(jnp.sum(e, axis=1, keepdims=True), 1e-30)
        o_cmp = _dot(p.astype(BF16), vcmp[g])
        mix_scr[g] = gate_scr[g, 0] * o_cmp + o_win[g]

        psum = p[0:tq] + p[tq:2 * tq] + p[2 * tq:3 * tq]
        c2s = c2s_ref[...]
        imp = sum(_nt(c2s, term) for term in _split3(psum))
        jb = lax.broadcasted_iota(jnp.int32, (n_sel, 1), 0)
        cur = t_lane // SEL_BLOCK
        imp = jnp.where((jb == 0) | (jb == cur) | (jb == cur - 1), FORCE_SCORE, imp)
        imp = jnp.where(jb > cur, -FORCE_SCORE, imp)
        rank = jnp.zeros(imp.shape, F32)
        for i in range(n_sel):
            ri = imp[i:i + 1, :]
            ahead = (ri > imp) | ((ri == imp) & (jb > i))
            rank = rank + jnp.where(ahead, 1.0, 0.0)
        sel_t = jnp.where(rank < float(min(SEL_TOP_N, n_sel)), 1.0, 0.0).astype(BF16)
        sel_t = jnp.concatenate([sel_t, jnp.zeros((LANE - n_sel, tq), BF16)], axis=0)
        sel = _nt(eye_ref[...], sel_t)
        bias = ((sel - 1.0) * (-SEL_NEG)).astype(BF16)
        for h in range(NSA_H):
            qaug[g, h * tq:(h + 1) * tq, LANE:2 * LANE] = bias

    last = (q0 + tq - 1) // tk

    def selected(n_chunks):
        lo = (n_chunks - 1) * tk
        kpos = lo + lax.broadcasted_iota(jnp.int32, (1, tk), 1)
        cbias = jnp.concatenate([jnp.where(kpos <= t1, 0.0, NEG)] * NSA_H, axis=0)
        heads = []
        sbs = [_nt(qaug[g], kaug[lo:lo + tk, :]) + cbias for g in range(NSA_G)]
        sas = [_nt(qaug[g], kaug[0:lo, :]) for g in range(NSA_G)] if lo else None
        for g in range(NSA_G):
            sb = sbs[g]
            blocks = [sb[:, j * LANE:(j + 1) * LANE] for j in range(tk // LANE)]
            if lo:
                sa = sas[g]
                blocks += [sa[:, j * LANE:(j + 1) * LANE] for j in range(lo // LANE)]
            m = jnp.max(functools.reduce(jnp.maximum, blocks), axis=1, keepdims=True)
            acc = _dot(jnp.exp2(sb - m).astype(BF16), vsel[g, lo:lo + tk, :])
            if lo:
                acc = acc + _dot(jnp.exp2(sa - m).astype(BF16), vsel[g, 0:lo, :])
            mix = mix_scr[g] + acc[:, 0:LANE] * (gate_scr[g, 1] / acc[:, LANE:2 * LANE])
            heads += [mix[h * tq:(h + 1) * tq] for h in range(NSA_H)]
        low = lax.broadcasted_iota(jnp.int32, (1, LANE), 1) < HEAD_DIM
        for j in range(len(heads) // 2):
            pair = jnp.where(low, heads[2 * j], heads[2 * j + 1])
            o_ref[:, j * LANE:(j + 1) * LANE] = pair.astype(o_ref.dtype)

    for n_chunks in range(1, kaug.shape[0] // tk + 1):
        pl.when(last == n_chunks - 1)(functools.partial(selected, n_chunks))


def _nsa(q, kc, sel, win, small, onehot, c2s_t, eye, perm, batch, seq):
    nq = seq // NSA_TQ
    nblk = seq // CMP_STRIDE
    rows3 = NSA_H * NSA_TQ
    qrow = lambda b, i: (b * nq + i, 0)
    per_b = lambda b, i: (b, 0)
    const = lambda b, i: (0, 0)
    return pl.pallas_call(
        _nsa_kernel,
        grid=(batch, nq),
        in_specs=[pl.BlockSpec((NSA_TQ, NSA_WIDTH), qrow),
                  pl.BlockSpec((nblk, 256), per_b),
                  pl.BlockSpec((seq, 256), per_b),
                  pl.BlockSpec((seq, 256), per_b),
                  pl.BlockSpec((seq, LANE), const),
                  pl.BlockSpec((NSA_TQ, LANE), qrow),
                  pl.BlockSpec(c2s_t.shape, const),
                  pl.BlockSpec((NSA_TQ, NSA_TQ), const),
                  pl.BlockSpec(perm.shape, lambda b, i: (0, 0, 0))],
        out_specs=pl.BlockSpec((NSA_TQ, NSA_WIDTH), qrow),
        out_shape=jax.ShapeDtypeStruct((batch * seq, NSA_WIDTH), BF16),
        scratch_shapes=[pltpu.VMEM((seq, 2 * LANE), BF16),
                        pltpu.VMEM((NSA_G, seq, 2 * LANE), BF16),
                        pltpu.VMEM((NSA_G, seq, 2 * LANE), BF16),
                        pltpu.VMEM((NSA_G, nblk, LANE), BF16),
                        pltpu.VMEM((NSA_G, rows3, 2 * LANE), BF16),
                        pltpu.VMEM((NSA_G, 3, rows3, LANE), F32),
                        pltpu.VMEM((NSA_G, rows3, LANE), F32)],
        compiler_params=pltpu.CompilerParams(dimension_semantics=("parallel", "arbitrary"),
                                             vmem_limit_bytes=VMEM_LIMIT),
        name="nsa_attention",
    )(q, kc, sel, win, onehot, small, c2s_t, eye, perm)


GLA_TILE = 256
GLA_PRE_TILES = 2


def _log_sigmoid(x):
    return jnp.minimum(x, 0.0) - jnp.log(1.0 + jnp.exp(-jnp.abs(x)))


def _gla_kernel(q_ref, k_ref, v_ref, og_ref, small_ref, wg_ref, bg_ref, ng_ref, ltri_ref, lsum_ref, bd_ref,
                o_ref, qd_s, ke_s, dec_s, oacc_s, st_s):
    seq = q_ref.shape[0]
    ch = GLA_CHUNK
    nh = GLA_HEADS
    tile = GLA_TILE
    qk_head = lax.broadcasted_iota(jnp.int32, (1, GLA_QK_PAD), 1) // GLA_DK
    low = lax.broadcasted_iota(jnp.int32, (1, LANE), 1) < GLA_DV
    ri = lax.broadcasted_iota(jnp.int32, (tile, 1), 0)
    cj = lax.broadcasted_iota(jnp.int32, (1, tile), 1)
    intra = (cj <= ri) & ((cj // ch) == (ri // ch))
    intra = jnp.concatenate([intra] * nh, axis=1)

    def pre(i, _):
        staged = [decays(GLA_PRE_TILES * i + j) for j in range(GLA_PRE_TILES)]
        for args in staged:
            intra_chunk(*args)
        return 0

    def decays(i):
        rs = pl.ds(pl.multiple_of(i * tile, tile), tile)
        x_hi, x_mid, _ = _split3(small_ref[rs, :])
        w_hi, w_mid, _ = _split3(wg_ref[...])
        z = _dot(x_hi, w_hi) + (_dot(x_hi, w_mid) + _dot(x_mid, w_hi)) + bg_ref[...]
        gl = _log_sigmoid(z) * (1.0 / GLA_GATE_NORM)
        g_hi, g_mid, _ = _split3(gl)
        bc = _dot(ltri_ref[...], g_hi) + _dot(ltri_ref[...], g_mid)
        bl = _dot(lsum_ref[...], g_hi) + _dot(lsum_ref[...], g_mid)
        dec = jnp.exp(bl)
        qd = (q_ref[rs, :].astype(F32) * (GLA_DK ** -0.5) * jnp.exp(bc)).astype(BF16)
        k_inv = k_ref[rs, :].astype(F32) * jnp.exp(-bc)
        ki = k_inv.astype(BF16)
        qd_s[rs, :] = qd
        ke_s[rs, :] = (k_inv * dec).astype(BF16)
        dec_s[rs, :] = dec
        return rs, qd, ki

    def intra_chunk(rs, qd, ki):
        k_heads = jnp.concatenate([jnp.where(qk_head == h, ki, jnp.zeros_like(ki)) for h in range(nh)], axis=0)
        a = jnp.where(intra, _nt(qd, k_heads), 0.0).astype(BF16)
        for pr in range(nh // 2):
            lanes = slice(pr * LANE, (pr + 1) * LANE)
            vp = v_ref[rs, lanes]
            zero = jnp.zeros_like(vp)
            v_pair = jnp.concatenate([jnp.where(low, vp, zero), jnp.where(low, zero, vp)], axis=0)
            oacc_s[rs, lanes] = _dot(a[:, 2 * pr * tile:(2 * pr + 2) * tile], v_pair)

    lax.fori_loop(0, seq // (GLA_PRE_TILES * tile), pre, 0)

    st_mask = (lax.broadcasted_iota(jnp.int32, (GLA_WIDTH, 1), 0) // GLA_DV) == qk_head
    st_s[...] = jnp.zeros(st_s.shape, F32)
    per_iter = tile // ch

    def scan(i, _):
        st = st_s[...]
        base = pl.multiple_of(i * tile, tile)
        dts = [_tn(v_ref[pl.ds(base + c * ch, ch), :], ke_s[pl.ds(base + c * ch, ch), :]) for c in range(per_iter)]
        for c in range(per_iter):
            rs = pl.ds(base + c * ch, ch)
            oacc_s[rs, :] = oacc_s[rs, :] + _nt(qd_s[rs, :], st.astype(BF16))
            st = st * dec_s[pl.ds(base + c * ch, 1), :] + jnp.where(st_mask, dts[c], 0.0)
        st_s[...] = st
        return 0

    lax.fori_loop(0, seq // tile, scan, 0)

    def post(i, _):
        tiles = [pl.ds(pl.multiple_of((2 * i + half) * tile, tile), tile) for half in range(2)]
        os_ = [oacc_s[rs, :] for rs in tiles]
        mss = [_dot((o * o).astype(BF16), bd_ref[...]) for o in os_]
        for rs, o, ms in zip(tiles, os_, mss):
            og = og_ref[rs, :].astype(F32)
            y = o * lax.rsqrt(ms + RMS_EPS) * ng_ref[...] * (og * jax.nn.sigmoid(og))
            o_ref[rs, :] = y.astype(o_ref.dtype)
        return 0

    lax.fori_loop(0, seq // (2 * tile), post, 0)


def _gla(gq, gk, gv, og, small, wg, bg, ng, ltri, lsum, bd, batch, seq):
    per_b = lambda b: (b, 0)
    const = lambda b: (0, 0)
    return pl.pallas_call(
        _gla_kernel,
        grid=(batch,),
        in_specs=[pl.BlockSpec((seq, GLA_QK_PAD), per_b), pl.BlockSpec((seq, GLA_QK_PAD), per_b),
                  pl.BlockSpec((seq, GLA_WIDTH), per_b), pl.BlockSpec((seq, GLA_WIDTH), per_b),
                  pl.BlockSpec((seq, LANE), per_b),
                  pl.BlockSpec(wg.shape, const), pl.BlockSpec(bg.shape, const), pl.BlockSpec(ng.shape, const),
                  pl.BlockSpec(ltri.shape, const), pl.BlockSpec(lsum.shape, const), pl.BlockSpec(bd.shape, const)],
        out_specs=pl.BlockSpec((seq, GLA_WIDTH), per_b),
        out_shape=jax.ShapeDtypeStruct((batch * seq, GLA_WIDTH), BF16),
        scratch_shapes=[pltpu.VMEM((seq, GLA_QK_PAD), BF16), pltpu.VMEM((seq, GLA_QK_PAD), BF16),
                        pltpu.VMEM((seq, GLA_QK_PAD), F32),
                        pltpu.VMEM((seq, GLA_WIDTH), F32), pltpu.VMEM((GLA_WIDTH, GLA_QK_PAD), F32)],
        compiler_params=pltpu.CompilerParams(dimension_semantics=("parallel",), vmem_limit_bytes=VMEM_LIMIT),
        name="gla",
    )(gq, gk, gv, og, small, wg, bg, ng, ltri, lsum, bd)


POOL_TILE = 256
POOL_HALO = 16


def _pool_kernel(u_ref, w_ref, scale_ref, o_ref):
    seq = u_ref.shape[0]
    lane = lax.broadcasted_iota(jnp.int32, (1, POOL_WIDTH), 1)
    grp = lane // POOL_GROUP
    win = jnp.where(grp == 0, POOL_WINDOWS[0],
                    jnp.where(grp == 1, POOL_WINDOWS[1], jnp.where(grp == 2, POOL_WINDOWS[2], POOL_WINDOWS[3])))
    for i in range(seq // POOL_TILE):
        r0 = i * POOL_TILE
        u = u_ref[r0:r0 + POOL_TILE, :].astype(F32)
        if i > 0:
            x = u_ref[r0 - POOL_HALO:r0 + POOL_TILE, :].astype(F32)
        else:
            x = jnp.concatenate([jnp.zeros((POOL_HALO, POOL_WIDTH), F32), u], axis=0)
        tot = x
        span = 1
        for gi, w in enumerate(POOL_WINDOWS):
            while span < w:
                x = x + pltpu.roll(x, span, 0)
                span *= 2
            tot = jnp.where(grp == gi, x, tot)
        tot = tot[POOL_HALO:, :]
        t = r0 + lax.broadcasted_iota(jnp.int32, (POOL_TILE, 1), 0)
        cnt = jnp.minimum(t + 1, win).astype(F32)
        pooled = tot / cnt - u
        o_ref[r0:r0 + POOL_TILE, :] = (_dot(pooled.astype(BF16), w_ref[...]) * scale_ref[...]).astype(o_ref.dtype)


def _pool(u, wbd, scale, batch, seq):
    per_b = lambda b: (b, 0)
    const = lambda b: (0, 0)
    return pl.pallas_call(
        _pool_kernel,
        grid=(batch,),
        in_specs=[pl.BlockSpec((seq, POOL_WIDTH), per_b),
                  pl.BlockSpec(wbd.shape, const), pl.BlockSpec(scale.shape, const)],
        out_specs=pl.BlockSpec((seq, POOL_WIDTH), per_b),
        out_shape=jax.ShapeDtypeStruct((batch * seq, POOL_WIDTH), BF16),
        compiler_params=pltpu.CompilerParams(dimension_semantics=("parallel",), vmem_limit_bytes=VMEM_LIMIT),
        name="pool",
    )(u, wbd, scale)


FFN_CHUNK = 1024


def _mix_ffn_kernel(nsa_ref, gla_ref, pool_ref, h_ref, wo_ref, g1_ref, b1_ref, wu_ref, wd_ref, g2_ref, b2_ref,
                    o_ref):
    half = h_ref.shape[0] // 2
    parts = [slice(0, half), slice(half, 2 * half)]
    mixed = []
    for rs in parts:
        acc = _dot(nsa_ref[rs, :], wo_ref[0:NSA_WIDTH, :])
        acc = acc + _dot(gla_ref[rs, :], wo_ref[NSA_WIDTH:NSA_WIDTH + GLA_WIDTH, :])
        mixed.append(acc + _dot(pool_ref[rs, :], wo_ref[NSA_WIDTH + GLA_WIDTH:, :]))
    hs = [_layer_norm(ALPHA * h_ref[rs, :] + a, g1_ref[...], b1_ref[...]) for rs, a in zip(parts, mixed)]
    hbs = [h.astype(BF16) for h in hs]
    accs = [ALPHA * h for h in hs]
    for c in range(D_FF // FFN_CHUNK):
        cs = slice(c * FFN_CHUNK, (c + 1) * FFN_CHUNK)
        us = [jnp.maximum(_dot(hb, wu_ref[:, cs]), 0.0) for hb in hbs]
        accs = [acc + _dot((u * u).astype(BF16), wd_ref[cs, :]) for acc, u in zip(accs, us)]
    for rs, acc in zip(parts, accs):
        o_ref[rs, :] = _layer_norm(acc, g2_ref[...], b2_ref[...])


def _mix_ffn(nsa, gla, pool, h, wo, g1, b1, wu, wd, g2, b2, layer, tm):
    m = h.shape[0]
    row = lambda i: (i, 0)
    const = lambda i: (0, 0)
    vec = pl.BlockSpec((1, D_MODEL), const)
    resident = lambda a: pl.BlockSpec((None,) + a.shape[1:], lambda i: (layer, 0, 0), pipeline_mode=pl.Buffered(1))
    return pl.pallas_call(
        _mix_ffn_kernel,
        grid=(m // tm,),
        in_specs=[pl.BlockSpec((tm, NSA_WIDTH), row), pl.BlockSpec((tm, GLA_WIDTH), row),
                  pl.BlockSpec((tm, POOL_WIDTH), row), pl.BlockSpec((tm, D_MODEL), row),
                  resident(wo), vec, vec, resident(wu), resident(wd), vec, vec],
        out_specs=pl.BlockSpec((tm, D_MODEL), row),
        out_shape=jax.ShapeDtypeStruct((m, D_MODEL), F32),
        compiler_params=pltpu.CompilerParams(dimension_semantics=("parallel",), vmem_limit_bytes=FFN_VMEM_LIMIT),
        name="out_proj_ffn",
    )(nsa, gla, pool, h, wo, g1, b1, wu, wd, g2, b2)


def _rope_tables(seq, scale):
    half = ROPE_DIM // 2
    inv = ROPE_THETA ** (-jnp.arange(half, dtype=F32) * 2.0 / ROPE_DIM)
    ang = jnp.arange(seq).astype(F32)[:, None] * inv[None, :]
    cos, sin = jnp.cos(ang), jnp.sin(ang)
    ones = jnp.ones((seq, HEAD_DIM - ROPE_DIM), F32)
    zeros = jnp.zeros((seq, HEAD_DIM - ROPE_DIM), F32)
    zh = jnp.zeros((seq, half), F32)
    a = jnp.concatenate([cos, cos, ones], axis=1)
    bm = jnp.concatenate([-sin, zh, zeros], axis=1)
    cm = jnp.concatenate([zh, sin, zeros], axis=1)
    reps = LANE // HEAD_DIM
    return tuple(jnp.tile(t, (1, reps)) * scale for t in (a, bm, cm))


def _static_tables(seq):
    n_sel = seq // SEL_BLOCK
    nblk = seq // CMP_STRIDE
    n_cmp = (seq - CMP_BLOCK) // CMP_STRIDE + 1
    onehot = np.zeros((seq, LANE), np.float32)
    onehot[np.arange(seq), np.arange(seq) // SEL_BLOCK] = 1.0
    c_lo = CMP_STRIDE * np.arange(n_cmp)[:, None]
    s_lo = SEL_BLOCK * np.arange(n_sel)[None, :]
    overlap = np.clip(np.minimum(c_lo + CMP_BLOCK, s_lo + SEL_BLOCK) - np.maximum(c_lo, s_lo), 0, None)
    c2s_t = np.zeros((n_sel, nblk), np.float32)
    c2s_t[:, :n_cmp] = (overlap.astype(np.float32) / CMP_BLOCK).T
    eye = np.eye(NSA_TQ, dtype=np.float32)
    perm = np.zeros((NSA_G, LANE, 2 * LANE), np.float32)
    for g in range(NSA_G):
        for rep in range(2):
            perm[g, g * HEAD_DIM + np.arange(HEAD_DIM), rep * HEAD_DIM + np.arange(HEAD_DIM)] = 1.0
    r = np.arange(GLA_TILE)
    same = (r[:, None] // GLA_CHUNK) == (r[None, :] // GLA_CHUNK)
    ltri = (same & (r[None, :] <= r[:, None])).astype(np.float32)
    lsum = same.astype(np.float32)
    hv = np.arange(GLA_WIDTH) // GLA_DV
    bd = (hv[:, None] == hv[None, :]).astype(np.float32) / GLA_DV
    as_bf = lambda a: jnp.asarray(a, BF16)
    return dict(onehot=as_bf(onehot), c2s_t=as_bf(c2s_t), eye=as_bf(eye), perm=as_bf(perm), ltri=as_bf(ltri),
                lsum=as_bf(lsum), bd=as_bf(bd))


def _block_diag2(w):
    z = jnp.zeros_like(w)
    return jnp.concatenate([jnp.concatenate([w, z], axis=-1), jnp.concatenate([z, w], axis=-1)], axis=-2)


def _layer_weights(w_in, cmp_pos, cmp_w1, cmp_w2, w_gate2, b_gate, norm_g, pool_w, pool_scale):
    o_nq, o_nkv, o_gate = 0, NSA_WIDTH, NSA_WIDTH + 768
    o_gq = o_gate + 18
    o_gk = o_gq + 192
    o_gv = o_gk + 192
    o_glr = o_gv + GLA_WIDTH
    o_og = o_glr + GLA_RANK
    o_pu = o_og + GLA_WIDTH
    p_sm, p_gq, p_gk, p_gv, p_og, p_pu = 9 * LANE, 10 * LANE, 12 * LANE, 14 * LANE, 17 * LANE, 20 * LANE
    segments = ((0, o_nq, o_gate - o_nq), (p_sm + SMALL_GATE0, o_gate, 18), (p_sm + SMALL_GLR0, o_glr, GLA_RANK),
                (p_gq, o_gq, 192), (p_gk, o_gk, 192), (p_gv, o_gv, GLA_WIDTH), (p_og, o_og, GLA_WIDTH),
                (p_pu, o_pu, POOL_WIDTH))
    wp = _permute_rows(w_in.T, segments)
    pos2 = jnp.concatenate([cmp_pos, cmp_pos], axis=-1)
    w1bd = _block_diag2(cmp_w1.reshape(2, CMP_BLOCK, HEAD_DIM, cmp_w1.shape[-1])).astype(BF16)
    w2bd = _block_diag2(cmp_w2).astype(BF16)
    wg = jnp.zeros((LANE, GLA_QK_PAD), F32).at[SMALL_GLR0:SMALL_GLR0 + GLA_RANK, :192].set(w_gate2)
    bg = jnp.zeros((1, GLA_QK_PAD), F32).at[0, :192].set(b_gate)
    ng = norm_g.reshape(1, GLA_WIDTH)
    pw = jnp.zeros((POOL_WIDTH, POOL_WIDTH), F32)
    for gi in range(len(POOL_WINDOWS)):
        sl = slice(gi * POOL_GROUP, (gi + 1) * POOL_GROUP)
        pw = pw.at[sl, sl].set(pool_w[gi])
    return dict(wp=wp, pos2=pos2, w1bd=w1bd, w2bd=w2bd, wg=wg, bg=bg, ng=ng, pw=pw.astype(BF16),
                ps=pool_scale.reshape(1, POOL_WIDTH))


@jax.jit
def kernel(x, w_in, cmp_pos, cmp_w1, cmp_w2, gla_w_gate2, gla_b_gate, gla_norm_g, pool_w, pool_scale,
           w_out, ln1_g, ln1_b, w_up, w_down, ln2_g, ln2_b):
    batch, seq, d = x.shape
    assert d == D_MODEL and seq % (GLA_PRE_TILES * GLA_TILE) == 0 and seq >= WIN_SPAN and seq // SEL_BLOCK <= LANE
    tm = 512
    st = _static_tables(seq)
    rope_q = _rope_tables(seq, HEAD_DIM ** -0.5 * LOG2E)
    rope_k = _rope_tables(seq, 1.0)
    wo_all, wu_all, wd_all = w_out.astype(BF16), w_up.astype(BF16), w_down.astype(BF16)
    h = x.reshape(batch * seq, d)
    for l in range(w_in.shape[0]):
        lw = _layer_weights(w_in[l], cmp_pos[l], cmp_w1[l], cmp_w2[l], gla_w_gate2[l], gla_b_gate[l],
                            gla_norm_g[l], pool_w[l], pool_scale[l])
        q, cmp_k, cmp_v, sel, win, small, gq, gk, gv, og, pu = _proj(h, lw["wp"], rope_q, rope_k, seq, tm)
        kc = _compress(cmp_k, cmp_v, lw["pos2"], lw["w1bd"], lw["w2bd"], batch, seq)
        nsa = _nsa(q, kc, sel, win, small, st["onehot"], st["c2s_t"], st["eye"], st["perm"], batch, seq)
        gla = _gla(gq, gk, gv, og, small, lw["wg"], lw["bg"], lw["ng"], st["ltri"], st["lsum"], st["bd"],
                   batch, seq)
        pool = _pool(pu, lw["pw"], lw["ps"], batch, seq)
        h = _mix_ffn(nsa, gla, pool, h, wo_all, ln1_g[l].reshape(1, d), ln1_b[l].reshape(1, d),
                     wu_all, wd_all, ln2_g[l].reshape(1, d), ln2_b[l].reshape(1, d), l, 2 * tm)
    return h.reshape(batch, seq, d)
```

```python
import functools
import numpy as np
import jax
import jax.numpy as jnp
from jax import lax
from jax.experimental import pallas as pl
from jax.experimental.pallas import tpu as pltpu

F32 = jnp.float32
BF16 = jnp.bfloat16

D_MODEL = 1024
HEAD_DIM = 64
NSA_G = 2
NSA_H = 3
NSA_WIDTH = NSA_G * NSA_H * HEAD_DIM
CMP_BLOCK = 32
CMP_STRIDE = 16
SEL_BLOCK = 64
SEL_TOP_N = 16
WINDOW = 512
FORCE_SCORE = 1.0e4
GLA_HEADS = 6
GLA_DK = 32
GLA_DV = 64
GLA_WIDTH = GLA_HEADS * GLA_DV
GLA_RANK = 16
GLA_GATE_NORM = 16.0
GLA_CHUNK = 64
GLA_QK_PAD = 256
POOL_WINDOWS = (2, 4, 8, 16)
POOL_GROUP = 64
POOL_WIDTH = 256
D_FF = 4 * D_MODEL
ROPE_THETA = 500000.0
ROPE_DIM = HEAD_DIM // 4
LN_EPS = 1e-5
RMS_EPS = 1e-6
DEPTH = 4
ALPHA = (2 * DEPTH) ** 0.25

LANE = 128
NEG = -1e30
SEL_NEG = -30000.0

VMEM_LIMIT = 48 * 1024 * 1024
FFN_VMEM_LIMIT = 54 * 1024 * 1024

SMALL_GATE0 = 0
SMALL_GLR0 = 32
PROJ_COLS = 22 * LANE


def _nt(a, b):
    return lax.dot_general(a, b, (((1,), (1,)), ((), ())), preferred_element_type=F32)


def _tn(a, b):
    return lax.dot_general(a, b, (((0,), (0,)), ((), ())), preferred_element_type=F32)


def _dot(a, b):
    return jnp.dot(a, b, preferred_element_type=F32)


def _split3(x):
    hi = x.astype(BF16)
    r = x - hi.astype(F32)
    mid = r.astype(BF16)
    lo = (r - mid.astype(F32)).astype(BF16)
    return hi, mid, lo


def _layer_norm(y, g, b):
    mu = jnp.mean(y, axis=-1, keepdims=True)
    d = y - mu
    var = jnp.mean(d * d, axis=-1, keepdims=True)
    return d * lax.rsqrt(var + LN_EPS) * g + b


def _rope(x, a, bm, cm):
    return x * a + pltpu.roll(x, LANE - ROPE_DIM // 2, 1) * bm + pltpu.roll(x, ROPE_DIM // 2, 1) * cm


def _proj_kernel(h_ref, w_ref, qa_ref, qb_ref, qc_ref, ka_ref, kb_ref, kc_ref,
                 q_out, cmpk_out, cmpv_out, sel_out, win_out, small_out, gq_out, gk_out, gv_out, og_out, pu_out):
    hb = h_ref[...].astype(BF16)
    plan = ([(q_out, j, "q") for j in range(3)]
            + [(cmpk_out, 0, "k"), (cmpv_out, 0, None), (sel_out, 0, "k"), (sel_out, 1, None),
               (win_out, 0, "k"), (win_out, 1, None), (small_out, 0, None)]
            + [(gq_out, j, None) for j in range(2)] + [(gk_out, j, None) for j in range(2)]
            + [(gv_out, j, None) for j in range(3)] + [(og_out, j, None) for j in range(3)]
            + [(pu_out, j, None) for j in range(2)])
    for c in range(0, len(plan), 2):
        acc = _nt(hb, w_ref[c * LANE:(c + 2) * LANE, :])
        for half in range(2):
            ref, j, kind = plan[c + half]
            x = acc[:, half * LANE:(half + 1) * LANE]
            if kind == "q":
                x = _rope(x, qa_ref[...], qb_ref[...], qc_ref[...])
            elif kind == "k":
                x = _rope(x, ka_ref[...], kb_ref[...], kc_ref[...])
            ref[:, j * LANE:(j + 1) * LANE] = x.astype(ref.dtype)


def _proj(h, w, rope_q, rope_k, seq, tm):
    m = h.shape[0]
    tiles_per_seq = seq // tm
    row = lambda i: (i, 0)
    tab = lambda i: (i % tiles_per_seq, 0)
    widths = [(NSA_WIDTH, BF16), (LANE, F32), (LANE, F32), (256, BF16), (256, BF16), (LANE, F32),
              (GLA_QK_PAD, BF16), (GLA_QK_PAD, BF16), (GLA_WIDTH, BF16), (GLA_WIDTH, BF16), (POOL_WIDTH, BF16)]
    return pl.pallas_call(
        _proj_kernel,
        grid=(m // tm,),
        in_specs=[pl.BlockSpec((tm, D_MODEL), row),
                  pl.BlockSpec((PROJ_COLS, D_MODEL), lambda i: (0, 0), pipeline_mode=pl.Buffered(1))]
                 + [pl.BlockSpec((tm, LANE), tab)] * 6,
        out_specs=[pl.BlockSpec((tm, wd), row) for wd, _ in widths],
        out_shape=[jax.ShapeDtypeStruct((m, wd), dt) for wd, dt in widths],
        compiler_params=pltpu.CompilerParams(dimension_semantics=("parallel",), vmem_limit_bytes=VMEM_LIMIT),
        name="in_proj",
    )(h, w, *rope_q, *rope_k)


def _permute_rows_kernel(segments, w_ref, o_ref, stage):
    stage[...] = jnp.zeros(stage.shape, F32)
    for dst, src, n in segments:
        stage[dst:dst + n, :] = w_ref[src:src + n, :]
    o_ref[...] = stage[...].astype(o_ref.dtype)


def _permute_rows(wt, segments):
    return pl.pallas_call(
        functools.partial(_permute_rows_kernel, segments),
        out_shape=jax.ShapeDtypeStruct((PROJ_COLS, D_MODEL), BF16),
        scratch_shapes=[pltpu.VMEM((PROJ_COLS, D_MODEL), F32)],
        compiler_params=pltpu.CompilerParams(vmem_limit_bytes=VMEM_LIMIT),
        name="in_proj_weights",
    )(wt)


def _cmp_kernel(k_ref, v_ref, pos_ref, w1_ref, w2_ref, o_ref):
    nblk = o_ref.shape[0]
    half = CMP_BLOCK // 2
    for kvi, x_ref in enumerate((k_ref, v_ref)):
        lanes = slice(kvi * LANE, (kvi + 1) * LANE)
        ha = jnp.zeros((nblk, LANE), F32)
        hb = jnp.zeros((nblk, LANE), F32)
        for r in range(half):
            t = x_ref[pl.ds(r, nblk, stride=CMP_STRIDE), :]
            ha = ha + _dot((t + pos_ref[kvi, r:r + 1, :]).astype(BF16), w1_ref[kvi, r])
            hb = hb + _dot((t + pos_ref[kvi, half + r:half + r + 1, :]).astype(BF16), w1_ref[kvi, half + r])
        hid = ha + pltpu.roll(hb, nblk - 1, 0)
        hid = jax.nn.gelu(hid, approximate=True)
        o_ref[:, lanes] = _dot(hid.astype(BF16), w2_ref[kvi])


def _compress(cmp_k, cmp_v, pos2, w1bd, w2bd, batch, seq):
    nblk = seq // CMP_STRIDE
    return pl.pallas_call(
        _cmp_kernel,
        grid=(batch,),
        in_specs=[pl.BlockSpec((seq, LANE), lambda b: (b, 0)), pl.BlockSpec((seq, LANE), lambda b: (b, 0)),
                  pl.BlockSpec((2, CMP_BLOCK, LANE), lambda b: (0, 0, 0)),
                  pl.BlockSpec((2, CMP_BLOCK, LANE, LANE), lambda b: (0, 0, 0, 0)),
                  pl.BlockSpec((2, LANE, LANE), lambda b: (0, 0, 0))],
        out_specs=pl.BlockSpec((nblk, 256), lambda b: (b, 0)),
        out_shape=jax.ShapeDtypeStruct((batch * nblk, 256), F32),
        compiler_params=pltpu.CompilerParams(dimension_semantics=("parallel",), vmem_limit_bytes=VMEM_LIMIT),
        name="nsa_compress",
    )(cmp_k, cmp_v, pos2, w1bd, w2bd)


NSA_TQ = 256
NSA_TK = 256
WIN_SPAN = WINDOW + NSA_TQ
LOG2E = 1.4426950408889634


def _nsa_kernel(q_ref, kc_ref, sel_ref, win_ref, oh_ref, g_ref, c2s_ref, eye_ref, perm_ref, o_ref,
                kaug, vsel, vwin, vcmp, qaug, gate_scr, mix_scr):
    tq, tk = NSA_TQ, NSA_TK
    rows3 = NSA_H * tq
    qi = pl.program_id(1)
    n_sel = oh_ref.shape[0] // SEL_BLOCK
    lane2 = lax.broadcasted_iota(jnp.int32, (1, 2 * LANE), 1)
    ones_half = jnp.where(lane2 >= LANE, 1.0, 0.0)

    @pl.when(qi == 0)
    def _():
        kaug[:, 0:LANE] = sel_ref[:, 0:LANE]
        kaug[:, LANE:2 * LANE] = oh_ref[...]
        for g in range(NSA_G):
            vsel[g] = (_dot(sel_ref[:, LANE:2 * LANE], perm_ref[g]) + ones_half).astype(BF16)
            vwin[g] = (_dot(win_ref[:, LANE:2 * LANE], perm_ref[g]) + ones_half).astype(BF16)
            vcmp[g] = _dot(kc_ref[:, LANE:2 * LANE].astype(BF16), perm_ref[g, :, 0:LANE]).astype(BF16)

    q0 = qi * tq
    t3 = q0 + (lax.broadcasted_iota(jnp.int32, (rows3, 1), 0) & (tq - 1))
    t_lane = q0 + lax.broadcasted_iota(jnp.int32, (1, tq), 1)
    gates = jax.nn.sigmoid(g_ref[...])
    for g in range(NSA_G):
        for h in range(NSA_H):
            for br in range(3):
                col = (g * NSA_H + h) * 3 + br
                gate_scr[g, br, h * tq:(h + 1) * tq, :] = jnp.broadcast_to(gates[:, col:col + 1], (tq, LANE))
    kc_b = kc_ref[:, 0:LANE].astype(BF16)
    for g in range(NSA_G):
        glanes = slice(g * HEAD_DIM, (g + 1) * HEAD_DIM)
        qaug[g, :, 0:LANE] = jnp.zeros((rows3, LANE), BF16)
        for h in range(NSA_H):
            c = (NSA_H * g + h) * HEAD_DIM
            qaug[g, h * tq:(h + 1) * tq, glanes] = q_ref[:, c:c + HEAD_DIM]

    ks = pl.multiple_of(jnp.maximum(q0 - WINDOW, 0), tq)
    kw = win_ref[pl.ds(ks, WIN_SPAN), 0:LANE]
    t1 = q0 + lax.broadcasted_iota(jnp.int32, (tq, 1), 0)
    diff = t1 - (ks + lax.broadcasted_iota(jnp.int32, (1, WIN_SPAN), 1))
    wbias = jnp.where((diff >= 0) & (diff < WINDOW), 0.0, NEG)
    wbias = jnp.concatenate([wbias] * NSA_H, axis=0)
    sws = [_nt(qaug[g, :, 0:LANE], kw) + wbias for g in range(NSA_G)]
    scs = [_nt(qaug[g, :, 0:LANE], kc_b) for g in range(NSA_G)]
    pws = [jnp.exp2(sw - jnp.max(sw, axis=1, keepdims=True)).astype(BF16) for sw in sws]
    a_wins = [_dot(pws[g], vwin[g, pl.ds(ks, WIN_SPAN), :]) for g in range(NSA_G)]
    o_win = [a[:, 0:LANE] * (gate_scr[g, 2] / a[:, LANE:2 * LANE]) for g, a in enumerate(a_wins)]

    for g in range(NSA_G):
        s = scs[g]
        ncol = lax.broadcasted_iota(jnp.int32, (1, s.shape[1]), 1)
        cmask = (CMP_STRIDE * ncol + (CMP_BLOCK - 1)) <= t3
        sm = jnp.where(cmask, s, NEG)
        m = jnp.max(sm, axis=1, keepdims=True)
        e = jnp.where(cmask, jnp.exp2(sm - m), 0.0)
        p = e / jnp.maximum(jnp.sum(e, axis=1, keepdims=True), 1e-30)
        o_cmp = _dot(p.astype(BF16), vcmp[g])
        mix_scr[g] = gate_scr[g, 0] * o_cmp + o_win[g]

        psum = p[0:tq] + p[tq:2 * tq] + p[2 * tq:3 * tq]
        c2s = c2s_ref[...]
        imp = sum(_nt(c2s, term) for term in _split3(psum))
        jb = lax.broadcasted_iota(jnp.int32, (n_sel, 1), 0)
        cur = t_lane // SEL_BLOCK
        imp = jnp.where((jb == 0) | (jb == cur) | (jb == cur - 1), FORCE_SCORE, imp)
        imp = jnp.where(jb > cur, -FORCE_SCORE, imp)
        rank = jnp.zeros(imp.shape, F32)
        for i in range(n_sel):
            ri = imp[i:i + 1, :]
            ahead = (ri > imp) | ((ri == imp) & (jb > i))
            rank = rank + jnp.where(ahead, 1.0, 0.0)
        sel_t = jnp.where(rank < float(min(SEL_TOP_N, n_sel)), 1.0, 0.0).astype(BF16)
        sel_t = jnp.concatenate([sel_t, jnp.zeros((LANE - n_sel, tq), BF16)], axis=0)
        sel = _nt(eye_ref[...], sel_t)
        bias = ((sel - 1.0) * (-SEL_NEG)).astype(BF16)
        for h in range(NSA_H):
            qaug[g, h * tq:(h + 1) * tq, LANE:2 * LANE] = bias

    last = (q0 + tq - 1) // tk

    def selected(n_chunks):
        lo = (n_chunks - 1) * tk
        kpos = lo + lax.broadcasted_iota(jnp.int32, (1, tk), 1)
        cbias = jnp.concatenate([jnp.where(kpos <= t1, 0.0, NEG)] * NSA_H, axis=0)
        heads = []
        sbs = [_nt(qaug[g], kaug[lo:lo + tk, :]) + cbias for g in range(NSA_G)]
        sas = [_nt(qaug[g], kaug[0:lo, :]) for g in range(NSA_G)] if lo else None
        for g in range(NSA_G):
            sb = sbs[g]
            blocks = [sb[:, j * LANE:(j + 1) * LANE] for j in range(tk // LANE)]
            if lo:
                sa = sas[g]
                blocks += [sa[:, j * LANE:(j + 1) * LANE] for j in range(lo // LANE)]
            m = jnp.max(functools.reduce(jnp.maximum, blocks), axis=1, keepdims=True)
            acc = _dot(jnp.exp2(sb - m).astype(BF16), vsel[g, lo:lo + tk, :])
            if lo:
                acc = acc + _dot(jnp.exp2(sa - m).astype(BF16), vsel[g, 0:lo, :])
            mix = mix_scr[g] + acc[:, 0:LANE] * (gate_scr[g, 1] / acc[:, LANE:2 * LANE])
            heads += [mix[h * tq:(h + 1) * tq] for h in range(NSA_H)]
        low = lax.broadcasted_iota(jnp.int32, (1, LANE), 1) < HEAD_DIM
        for j in range(len(heads) // 2):
            pair = jnp.where(low, heads[2 * j], heads[2 * j + 1])
            o_ref[:, j * LANE:(j + 1) * LANE] = pair.astype(o_ref.dtype)

    for n_chunks in range(1, kaug.shape[0] // tk + 1):
        pl.when(last == n_chunks - 1)(functools.partial(selected, n_chunks))


def _nsa(q, kc, sel, win, small, onehot, c2s_t, eye, perm, batch, seq):
    nq = seq // NSA_TQ
    nblk = seq // CMP_STRIDE
    rows3 = NSA_H * NSA_TQ
    qrow = lambda b, i: (b * nq + i, 0)
    per_b = lambda b, i: (b, 0)
    const = lambda b, i: (0, 0)
    return pl.pallas_call(
        _nsa_kernel,
        grid=(batch, nq),
        in_specs=[pl.BlockSpec((NSA_TQ, NSA_WIDTH), qrow),
                  pl.BlockSpec((nblk, 256), per_b),
                  pl.BlockSpec((seq, 256), per_b),
                  pl.BlockSpec((seq, 256), per_b),
                  pl.BlockSpec((seq, LANE), const),
                  pl.BlockSpec((NSA_TQ, LANE), qrow),
                  pl.BlockSpec(c2s_t.shape, const),
                  pl.BlockSpec((NSA_TQ, NSA_TQ), const),
                  pl.BlockSpec(perm.shape, lambda b, i: (0, 0, 0))],
        out_specs=pl.BlockSpec((NSA_TQ, NSA_WIDTH), qrow),
        out_shape=jax.ShapeDtypeStruct((batch * seq, NSA_WIDTH), BF16),
        scratch_shapes=[pltpu.VMEM((seq, 2 * LANE), BF16),
                        pltpu.VMEM((NSA_G, seq, 2 * LANE), BF16),
                        pltpu.VMEM((NSA_G, seq, 2 * LANE), BF16),
                        pltpu.VMEM((NSA_G, nblk, LANE), BF16),
                        pltpu.VMEM((NSA_G, rows3, 2 * LANE), BF16),
                        pltpu.VMEM((NSA_G, 3, rows3, LANE), F32),
                        pltpu.VMEM((NSA_G, rows3, LANE), F32)],
        compiler_params=pltpu.CompilerParams(dimension_semantics=("parallel", "arbitrary"),
                                             vmem_limit_bytes=VMEM_LIMIT),
        name="nsa_attention",
    )(q, kc, sel, win, onehot, small, c2s_t, eye, perm)


GLA_TILE = 256
GLA_PRE_TILES = 2
GLA_SCAN_CHUNKS = 8


def _log_sigmoid(x):
    return jnp.minimum(x, 0.0) - jnp.log(1.0 + jnp.exp(-jnp.abs(x)))


def _gla_kernel(q_ref, k_ref, v_ref, og_ref, small_ref, wg_ref, bg_ref, ng_ref, ltri_ref, lsum_ref, bd_ref,
                o_ref, qd_s, ke_s, dec_s, oacc_s, st_s):
    seq = q_ref.shape[0]
    ch = GLA_CHUNK
    nh = GLA_HEADS
    tile = GLA_TILE
    qk_head = lax.broadcasted_iota(jnp.int32, (1, GLA_QK_PAD), 1) // GLA_DK
    low = lax.broadcasted_iota(jnp.int32, (1, LANE), 1) < GLA_DV
    ri = lax.broadcasted_iota(jnp.int32, (tile, 1), 0)
    cj = lax.broadcasted_iota(jnp.int32, (1, tile), 1)
    intra = (cj <= ri) & ((cj // ch) == (ri // ch))
    intra = jnp.concatenate([intra] * nh, axis=1)

    def pre(i, _):
        staged = [decays(GLA_PRE_TILES * i + j) for j in range(GLA_PRE_TILES)]
        for args in staged:
            intra_chunk(*args)
        return 0

    def decays(i):
        rs = pl.ds(pl.multiple_of(i * tile, tile), tile)
        x_hi, x_mid, _ = _split3(small_ref[rs, :])
        w_hi, w_mid, _ = _split3(wg_ref[...])
        z = _dot(x_hi, w_hi) + (_dot(x_hi, w_mid) + _dot(x_mid, w_hi)) + bg_ref[...]
        gl = _log_sigmoid(z) * (1.0 / GLA_GATE_NORM)
        g_hi, g_mid, _ = _split3(gl)
        bc = _dot(ltri_ref[...], g_hi) + _dot(ltri_ref[...], g_mid)
        bl = _dot(lsum_ref[...], g_hi) + _dot(lsum_ref[...], g_mid)
        dec = jnp.exp(bl)
        qd = (q_ref[rs, :].astype(F32) * (GLA_DK ** -0.5) * jnp.exp(bc)).astype(BF16)
        k_inv = k_ref[rs, :].astype(F32) * jnp.exp(-bc)
        ki = k_inv.astype(BF16)
        qd_s[rs, :] = qd
        ke_s[rs, :] = (k_inv * dec).astype(BF16)
        dec_s[rs, :] = dec
        return rs, qd, ki

    def intra_chunk(rs, qd, ki):
        k_heads = jnp.concatenate([jnp.where(qk_head == h, ki, jnp.zeros_like(ki)) for h in range(nh)], axis=0)
        a = jnp.where(intra, _nt(qd, k_heads), 0.0).astype(BF16)
        for pr in range(nh // 2):
            lanes = slice(pr * LANE, (pr + 1) * LANE)
            vp = v_ref[rs, lanes]
            zero = jnp.zeros_like(vp)
            v_pair = jnp.concatenate([jnp.where(low, vp, zero), jnp.where(low, zero, vp)], axis=0)
            oacc_s[rs, lanes] = _dot(a[:, 2 * pr * tile:(2 * pr + 2) * tile], v_pair)

    lax.fori_loop(0, seq // (GLA_PRE_TILES * tile), pre, 0)

    st_mask = (lax.broadcasted_iota(jnp.int32, (GLA_WIDTH, 1), 0) // GLA_DV) == qk_head
    st_s[...] = jnp.zeros(st_s.shape, F32)
    per_iter = GLA_SCAN_CHUNKS
    span = per_iter * ch

    def scan(i, _):
        st = st_s[...]
        base = pl.multiple_of(i * span, span)
        dts = [_tn(v_ref[pl.ds(base + c * ch, ch), :], ke_s[pl.ds(base + c * ch, ch), :]) for c in range(per_iter)]
        for c in range(per_iter):
            rs = pl.ds(base + c * ch, ch)
            oacc_s[rs, :] = oacc_s[rs, :] + _nt(qd_s[rs, :], st.astype(BF16))
            st = st * dec_s[pl.ds(base + c * ch, 1), :] + jnp.where(st_mask, dts[c], 0.0)
        st_s[...] = st
        return 0

    lax.fori_loop(0, seq // span, scan, 0)

    def post(i, _):
        tiles = [pl.ds(pl.multiple_of((2 * i + half) * tile, tile), tile) for half in range(2)]
        os_ = [oacc_s[rs, :] for rs in tiles]
        mss = [_dot((o * o).astype(BF16), bd_ref[...]) for o in os_]
        for rs, o, ms in zip(tiles, os_, mss):
            og = og_ref[rs, :].astype(F32)
            y = o * lax.rsqrt(ms + RMS_EPS) * ng_ref[...] * (og * jax.nn.sigmoid(og))
            o_ref[rs, :] = y.astype(o_ref.dtype)
        return 0

    lax.fori_loop(0, seq // (2 * tile), post, 0)


def _gla(gq, gk, gv, og, small, wg, bg, ng, ltri, lsum, bd, batch, seq):
    per_b = lambda b: (b, 0)
    const = lambda b: (0, 0)
    return pl.pallas_call(
        _gla_kernel,
        grid=(batch,),
        in_specs=[pl.BlockSpec((seq, GLA_QK_PAD), per_b), pl.BlockSpec((seq, GLA_QK_PAD), per_b),
                  pl.BlockSpec((seq, GLA_WIDTH), per_b), pl.BlockSpec((seq, GLA_WIDTH), per_b),
                  pl.BlockSpec((seq, LANE), per_b),
                  pl.BlockSpec(wg.shape, const), pl.BlockSpec(bg.shape, const), pl.BlockSpec(ng.shape, const),
                  pl.BlockSpec(ltri.shape, const), pl.BlockSpec(lsum.shape, const), pl.BlockSpec(bd.shape, const)],
        out_specs=pl.BlockSpec((seq, GLA_WIDTH), per_b),
        out_shape=jax.ShapeDtypeStruct((batch * seq, GLA_WIDTH), BF16),
        scratch_shapes=[pltpu.VMEM((seq, GLA_QK_PAD), BF16), pltpu.VMEM((seq, GLA_QK_PAD), BF16),
                        pltpu.VMEM((seq, GLA_QK_PAD), F32),
                        pltpu.VMEM((seq, GLA_WIDTH), F32), pltpu.VMEM((GLA_WIDTH, GLA_QK_PAD), F32)],
        compiler_params=pltpu.CompilerParams(dimension_semantics=("parallel",), vmem_limit_bytes=VMEM_LIMIT),
        name="gla",
    )(gq, gk, gv, og, small, wg, bg, ng, ltri, lsum, bd)


POOL_TILE = 256
POOL_HALO = 16


def _pool_kernel(u_ref, w_ref, scale_ref, o_ref):
    seq = u_ref.shape[0]
    lane = lax.broadcasted_iota(jnp.int32, (1, POOL_WIDTH), 1)
    grp = lane // POOL_GROUP
    win = jnp.where(grp == 0, POOL_WINDOWS[0],
                    jnp.where(grp == 1, POOL_WINDOWS[1], jnp.where(grp == 2, POOL_WINDOWS[2], POOL_WINDOWS[3])))
    for i in range(seq // POOL_TILE):
        r0 = i * POOL_TILE
        u = u_ref[r0:r0 + POOL_TILE, :].astype(F32)
        if i > 0:
            x = u_ref[r0 - POOL_HALO:r0 + POOL_TILE, :].astype(F32)
        else:
            x = jnp.concatenate([jnp.zeros((POOL_HALO, POOL_WIDTH), F32), u], axis=0)
        tot = x
        span = 1
        for gi, w in enumerate(POOL_WINDOWS):
            while span < w:
                x = x + pltpu.roll(x, span, 0)
                span *= 2
            tot = jnp.where(grp == gi, x, tot)
        tot = tot[POOL_HALO:, :]
        t = r0 + lax.broadcasted_iota(jnp.int32, (POOL_TILE, 1), 0)
        cnt = jnp.minimum(t + 1, win).astype(F32)
        pooled = tot / cnt - u
        o_ref[r0:r0 + POOL_TILE, :] = (_dot(pooled.astype(BF16), w_ref[...]) * scale_ref[...]).astype(o_ref.dtype)


def _pool(u, wbd, scale, batch, seq):
    per_b = lambda b: (b, 0)
    const = lambda b: (0, 0)
    return pl.pallas_call(
        _pool_kernel,
        grid=(batch,),
        in_specs=[pl.BlockSpec((seq, POOL_WIDTH), per_b),
                  pl.BlockSpec(wbd.shape, const), pl.BlockSpec(scale.shape, const)],
        out_specs=pl.BlockSpec((seq, POOL_WIDTH), per_b),
        out_shape=jax.ShapeDtypeStruct((batch * seq, POOL_WIDTH), BF16),
        compiler_params=pltpu.CompilerParams(dimension_semantics=("parallel",), vmem_limit_bytes=VMEM_LIMIT),
        name="pool",
    )(u, wbd, scale)


FFN_CHUNK = 1024


def _mix_ffn_kernel(nsa_ref, gla_ref, pool_ref, h_ref, wo_ref, g1_ref, b1_ref, wu_ref, wd_ref, g2_ref, b2_ref,
                    o_ref):
    half = h_ref.shape[0] // 2
    parts = [slice(0, half), slice(half, 2 * half)]
    mixed = []
    for rs in parts:
        acc = _dot(nsa_ref[rs, :], wo_ref[0:NSA_WIDTH, :])
        acc = acc + _dot(gla_ref[rs, :], wo_ref[NSA_WIDTH:NSA_WIDTH + GLA_WIDTH, :])
        mixed.append(acc + _dot(pool_ref[rs, :], wo_ref[NSA_WIDTH + GLA_WIDTH:, :]))
    hs = [_layer_norm(ALPHA * h_ref[rs, :] + a, g1_ref[...], b1_ref[...]) for rs, a in zip(parts, mixed)]
    hbs = [h.astype(BF16) for h in hs]
    accs = [ALPHA * h for h in hs]
    for c in range(D_FF // FFN_CHUNK):
        cs = slice(c * FFN_CHUNK, (c + 1) * FFN_CHUNK)
        us = [jnp.maximum(_dot(hb, wu_ref[:, cs]), 0.0) for hb in hbs]
        accs = [acc + _dot((u * u).astype(BF16), wd_ref[cs, :]) for acc, u in zip(accs, us)]
    for rs, acc in zip(parts, accs):
        o_ref[rs, :] = _layer_norm(acc, g2_ref[...], b2_ref[...])


def _mix_ffn(nsa, gla, pool, h, wo, g1, b1, wu, wd, g2, b2, layer, tm):
    m = h.shape[0]
    row = lambda i: (i, 0)
    const = lambda i: (0, 0)
    vec = pl.BlockSpec((1, D_MODEL), const)
    resident = lambda a: pl.BlockSpec((None,) + a.shape[1:], lambda i: (layer, 0, 0), pipeline_mode=pl.Buffered(1))
    return pl.pallas_call(
        _mix_ffn_kernel,
        grid=(m // tm,),
        in_specs=[pl.BlockSpec((tm, NSA_WIDTH), row), pl.BlockSpec((tm, GLA_WIDTH), row),
                  pl.BlockSpec((tm, POOL_WIDTH), row), pl.BlockSpec((tm, D_MODEL), row),
                  resident(wo), vec, vec, resident(wu), resident(wd), vec, vec],
        out_specs=pl.BlockSpec((tm, D_MODEL), row),
        out_shape=jax.ShapeDtypeStruct((m, D_MODEL), F32),
        compiler_params=pltpu.CompilerParams(dimension_semantics=("parallel",), vmem_limit_bytes=FFN_VMEM_LIMIT),
        name="out_proj_ffn",
    )(nsa, gla, pool, h, wo, g1, b1, wu, wd, g2, b2)


def _rope_tables(seq, scale):
    half = ROPE_DIM // 2
    inv = ROPE_THETA ** (-jnp.arange(half, dtype=F32) * 2.0 / ROPE_DIM)
    ang = jnp.arange(seq).astype(F32)[:, None] * inv[None, :]
    cos, sin = jnp.cos(ang), jnp.sin(ang)
    ones = jnp.ones((seq, HEAD_DIM - ROPE_DIM), F32)
    zeros = jnp.zeros((seq, HEAD_DIM - ROPE_DIM), F32)
    zh = jnp.zeros((seq, half), F32)
    a = jnp.concatenate([cos, cos, ones], axis=1)
    bm = jnp.concatenate([-sin, zh, zeros], axis=1)
    cm = jnp.concatenate([zh, sin, zeros], axis=1)
    reps = LANE // HEAD_DIM
    return tuple(jnp.tile(t, (1, reps)) * scale for t in (a, bm, cm))


def _static_tables(seq):
    n_sel = seq // SEL_BLOCK
    nblk = seq // CMP_STRIDE
    n_cmp = (seq - CMP_BLOCK) // CMP_STRIDE + 1
    onehot = np.zeros((seq, LANE), np.float32)
    onehot[np.arange(seq), np.arange(seq) // SEL_BLOCK] = 1.0
    c_lo = CMP_STRIDE * np.arange(n_cmp)[:, None]
    s_lo = SEL_BLOCK * np.arange(n_sel)[None, :]
    overlap = np.clip(np.minimum(c_lo + CMP_BLOCK, s_lo + SEL_BLOCK) - np.maximum(c_lo, s_lo), 0, None)
    c2s_t = np.zeros((n_sel, nblk), np.float32)
    c2s_t[:, :n_cmp] = (overlap.astype(np.float32) / CMP_BLOCK).T
    eye = np.eye(NSA_TQ, dtype=np.float32)
    perm = np.zeros((NSA_G, LANE, 2 * LANE), np.float32)
    for g in range(NSA_G):
        for rep in range(2):
            perm[g, g * HEAD_DIM + np.arange(HEAD_DIM), rep * HEAD_DIM + np.arange(HEAD_DIM)] = 1.0
    r = np.arange(GLA_TILE)
    same = (r[:, None] // GLA_CHUNK) == (r[None, :] // GLA_CHUNK)
    ltri = (same & (r[None, :] <= r[:, None])).astype(np.float32)
    lsum = same.astype(np.float32)
    hv = np.arange(GLA_WIDTH) // GLA_DV
    bd = (hv[:, None] == hv[None, :]).astype(np.float32) / GLA_DV
    as_bf = lambda a: jnp.asarray(a, BF16)
    return dict(onehot=as_bf(onehot), c2s_t=as_bf(c2s_t), eye=as_bf(eye), perm=as_bf(perm), ltri=as_bf(ltri),
                lsum=as_bf(lsum), bd=as_bf(bd))


def _block_diag2(w):
    z = jnp.zeros_like(w)
    return jnp.concatenate([jnp.concatenate([w, z], axis=-1), jnp.concatenate([z, w], axis=-1)], axis=-2)


def _layer_weights(w_in, cmp_pos, cmp_w1, cmp_w2, w_gate2, b_gate, norm_g, pool_w, pool_scale):
    o_nq, o_nkv, o_gate = 0, NSA_WIDTH, NSA_WIDTH + 768
    o_gq = o_gate + 18
    o_gk = o_gq + 192
    o_gv = o_gk + 192
    o_glr = o_gv + GLA_WIDTH
    o_og = o_glr + GLA_RANK
    o_pu = o_og + GLA_WIDTH
    p_sm, p_gq, p_gk, p_gv, p_og, p_pu = 9 * LANE, 10 * LANE, 12 * LANE, 14 * LANE, 17 * LANE, 20 * LANE
    segments = ((0, o_nq, o_gate - o_nq), (p_sm + SMALL_GATE0, o_gate, 18), (p_sm + SMALL_GLR0, o_glr, GLA_RANK),
                (p_gq, o_gq, 192), (p_gk, o_gk, 192), (p_gv, o_gv, GLA_WIDTH), (p_og, o_og, GLA_WIDTH),
                (p_pu, o_pu, POOL_WIDTH))
    wp = _permute_rows(w_in.T, segments)
    pos2 = jnp.concatenate([cmp_pos, cmp_pos], axis=-1)
    w1bd = _block_diag2(cmp_w1.reshape(2, CMP_BLOCK, HEAD_DIM, cmp_w1.shape[-1])).astype(BF16)
    w2bd = _block_diag2(cmp_w2).astype(BF16)
    wg = jnp.zeros((LANE, GLA_QK_PAD), F32).at[SMALL_GLR0:SMALL_GLR0 + GLA_RANK, :192].set(w_gate2)
    bg = jnp.zeros((1, GLA_QK_PAD), F32).at[0, :192].set(b_gate)
    ng = norm_g.reshape(1, GLA_WIDTH)
    pw = jnp.zeros((POOL_WIDTH, POOL_WIDTH), F32)
    for gi in range(len(POOL_WINDOWS)):
        sl = slice(gi * POOL_GROUP, (gi + 1) * POOL_GROUP)
        pw = pw.at[sl, sl].set(pool_w[gi])
    return dict(wp=wp, pos2=pos2, w1bd=w1bd, w2bd=w2bd, wg=wg, bg=bg, ng=ng, pw=pw.astype(BF16),
                ps=pool_scale.reshape(1, POOL_WIDTH))


@jax.jit
def kernel(x, w_in, cmp_pos, cmp_w1, cmp_w2, gla_w_gate2, gla_b_gate, gla_norm_g, pool_w, pool_scale,
           w_out, ln1_g, ln1_b, w_up, w_down, ln2_g, ln2_b):
    batch, seq, d = x.shape
    assert d == D_MODEL and seq % (GLA_PRE_TILES * GLA_TILE) == 0 and seq >= WIN_SPAN and seq // SEL_BLOCK <= LANE
    tm = 512
    st = _static_tables(seq)
    rope_q = _rope_tables(seq, HEAD_DIM ** -0.5 * LOG2E)
    rope_k = _rope_tables(seq, 1.0)
    wo_all, wu_all, wd_all = w_out.astype(BF16), w_up.astype(BF16), w_down.astype(BF16)
    h = x.reshape(batch * seq, d)
    for l in range(w_in.shape[0]):
        lw = _layer_weights(w_in[l], cmp_pos[l], cmp_w1[l], cmp_w2[l], gla_w_gate2[l], gla_b_gate[l],
                            gla_norm_g[l], pool_w[l], pool_scale[l])
        q, cmp_k, cmp_v, sel, win, small, gq, gk, gv, og, pu = _proj(h, lw["wp"], rope_q, rope_k, seq, 2 * tm)
        kc = _compress(cmp_k, cmp_v, lw["pos2"], lw["w1bd"], lw["w2bd"], batch, seq)
        nsa = _nsa(q, kc, sel, win, small, st["onehot"], st["c2s_t"], st["eye"], st["perm"], batch, seq)
        gla = _gla(gq, gk, gv, og, small, lw["wg"], lw["bg"], lw["ng"], st["ltri"], st["lsum"], st["bd"],
                   batch, seq)
        pool = _pool(pu, lw["pw"], lw["ps"], batch, seq)
        h = _mix_ffn(nsa, gla, pool, h, wo_all, ln1_g[l].reshape(1, d), ln1_b[l].reshape(1, d),
                     wu_all, wd_all, ln2_g[l].reshape(1, d), ln2_b[l].reshape(1, d), l, 2 * tm)
    return h.reshape(batch, seq, d)
```

```python
import functools
import numpy as np
import jax
import jax.numpy as jnp
from jax import lax
from jax.experimental import pallas as pl
from jax.experimental.pallas import tpu as pltpu

F32 = jnp.float32
BF16 = jnp.bfloat16

D_MODEL = 1024
HEAD_DIM = 64
NSA_G = 2
NSA_H = 3
NSA_WIDTH = NSA_G * NSA_H * HEAD_DIM
CMP_BLOCK = 32
CMP_STRIDE = 16
SEL_BLOCK = 64
SEL_TOP_N = 16
WINDOW = 512
FORCE_SCORE = 1.0e4
GLA_HEADS = 6
GLA_DK = 32
GLA_DV = 64
GLA_WIDTH = GLA_HEADS * GLA_DV
GLA_RANK = 16
GLA_GATE_NORM = 16.0
GLA_CHUNK = 64
GLA_QK_PAD = 256
POOL_WINDOWS = (2, 4, 8, 16)
POOL_GROUP = 64
POOL_WIDTH = 256
D_FF = 4 * D_MODEL
ROPE_THETA = 500000.0
ROPE_DIM = HEAD_DIM // 4
LN_EPS = 1e-5
RMS_EPS = 1e-6
DEPTH = 4
ALPHA = (2 * DEPTH) ** 0.25

LANE = 128
NEG = -1e30
SEL_NEG = -30000.0

VMEM_LIMIT = 48 * 1024 * 1024
FFN_VMEM_LIMIT = 54 * 1024 * 1024

SMALL_GATE0 = 0
SMALL_GLR0 = 32
PROJ_COLS = 22 * LANE


def _nt(a, b):
    return lax.dot_general(a, b, (((1,), (1,)), ((), ())), preferred_element_type=F32)


def _tn(a, b):
    return lax.dot_general(a, b, (((0,), (0,)), ((), ())), preferred_element_type=F32)


def _dot(a, b):
    return jnp.dot(a, b, preferred_element_type=F32)


def _split3(x):
    hi = x.astype(BF16)
    r = x - hi.astype(F32)
    mid = r.astype(BF16)
    lo = (r - mid.astype(F32)).astype(BF16)
    return hi, mid, lo


def _layer_norm(y, g, b):
    mu = jnp.mean(y, axis=-1, keepdims=True)
    d = y - mu
    var = jnp.mean(d * d, axis=-1, keepdims=True)
    return d * lax.rsqrt(var + LN_EPS) * g + b


def _rope(x, a, bm, cm):
    return x * a + pltpu.roll(x, LANE - ROPE_DIM // 2, 1) * bm + pltpu.roll(x, ROPE_DIM // 2, 1) * cm


def _proj_kernel(h_ref, w_ref, qa_ref, qb_ref, qc_ref, ka_ref, kb_ref, kc_ref,
                 q_out, cmpk_out, cmpv_out, sel_out, win_out, small_out, gq_out, gk_out, gv_out, og_out, pu_out):
    hb = h_ref[...].astype(BF16)
    plan = ([(q_out, j, "q") for j in range(3)]
            + [(cmpk_out, 0, "k"), (cmpv_out, 0, None), (sel_out, 0, "k"), (sel_out, 1, None),
               (win_out, 0, "k"), (win_out, 1, None), (small_out, 0, None)]
            + [(gq_out, j, None) for j in range(2)] + [(gk_out, j, None) for j in range(2)]
            + [(gv_out, j, None) for j in range(3)] + [(og_out, j, None) for j in range(3)]
            + [(pu_out, j, None) for j in range(2)])
    for c in range(0, len(plan), 2):
        acc = _nt(hb, w_ref[c * LANE:(c + 2) * LANE, :])
        for half in range(2):
            ref, j, kind = plan[c + half]
            x = acc[:, half * LANE:(half + 1) * LANE]
            if kind == "q":
                x = _rope(x, qa_ref[...], qb_ref[...], qc_ref[...])
            elif kind == "k":
                x = _rope(x, ka_ref[...], kb_ref[...], kc_ref[...])
            ref[:, j * LANE:(j + 1) * LANE] = x.astype(ref.dtype)


def _proj(h, w, rope_q, rope_k, seq, tm):
    m = h.shape[0]
    tiles_per_seq = seq // tm
    row = lambda i: (i, 0)
    tab = lambda i: (i % tiles_per_seq, 0)
    widths = [(NSA_WIDTH, BF16), (LANE, F32), (LANE, F32), (256, BF16), (256, BF16), (LANE, F32),
              (GLA_QK_PAD, BF16), (GLA_QK_PAD, BF16), (GLA_WIDTH, BF16), (GLA_WIDTH, BF16), (POOL_WIDTH, BF16)]
    return pl.pallas_call(
        _proj_kernel,
        grid=(m // tm,),
        in_specs=[pl.BlockSpec((tm, D_MODEL), row),
                  pl.BlockSpec((PROJ_COLS, D_MODEL), lambda i: (0, 0), pipeline_mode=pl.Buffered(1))]
                 + [pl.BlockSpec((tm, LANE), tab)] * 6,
        out_specs=[pl.BlockSpec((tm, wd), row) for wd, _ in widths],
        out_shape=[jax.ShapeDtypeStruct((m, wd), dt) for wd, dt in widths],
        compiler_params=pltpu.CompilerParams(dimension_semantics=("parallel",), vmem_limit_bytes=VMEM_LIMIT),
        name="in_proj",
    )(h, w, *rope_q, *rope_k)


def _permute_rows_kernel(segments, w_ref, o_ref, stage):
    stage[...] = jnp.zeros(stage.shape, F32)
    for dst, src, n in segments:
        stage[dst:dst + n, :] = w_ref[src:src + n, :]
    o_ref[...] = stage[...].astype(o_ref.dtype)


def _permute_rows(wt, segments):
    return pl.pallas_call(
        functools.partial(_permute_rows_kernel, segments),
        out_shape=jax.ShapeDtypeStruct((PROJ_COLS, D_MODEL), BF16),
        scratch_shapes=[pltpu.VMEM((PROJ_COLS, D_MODEL), F32)],
        compiler_params=pltpu.CompilerParams(vmem_limit_bytes=VMEM_LIMIT),
        name="in_proj_weights",
    )(wt)


def _cmp_kernel(k_ref, v_ref, pos_ref, w1_ref, w2_ref, o_ref):
    nblk = o_ref.shape[0]
    half = CMP_BLOCK // 2
    for kvi, x_ref in enumerate((k_ref, v_ref)):
        lanes = slice(kvi * LANE, (kvi + 1) * LANE)
        ha = jnp.zeros((nblk, LANE), F32)
        hb = jnp.zeros((nblk, LANE), F32)
        for r in range(half):
            t = x_ref[pl.ds(r, nblk, stride=CMP_STRIDE), :]
            ha = ha + _dot((t + pos_ref[kvi, r:r + 1, :]).astype(BF16), w1_ref[kvi, r])
            hb = hb + _dot((t + pos_ref[kvi, half + r:half + r + 1, :]).astype(BF16), w1_ref[kvi, half + r])
        hid = ha + pltpu.roll(hb, nblk - 1, 0)
        hid = jax.nn.gelu(hid, approximate=True)
        o_ref[:, lanes] = _dot(hid.astype(BF16), w2_ref[kvi])


def _compress(cmp_k, cmp_v, pos2, w1bd, w2bd, batch, seq):
    nblk = seq // CMP_STRIDE
    return pl.pallas_call(
        _cmp_kernel,
        grid=(batch,),
        in_specs=[pl.BlockSpec((seq, LANE), lambda b: (b, 0)), pl.BlockSpec((seq, LANE), lambda b: (b, 0)),
                  pl.BlockSpec((2, CMP_BLOCK, LANE), lambda b: (0, 0, 0)),
                  pl.BlockSpec((2, CMP_BLOCK, LANE, LANE), lambda b: (0, 0, 0, 0)),
                  pl.BlockSpec((2, LANE, LANE), lambda b: (0, 0, 0))],
        out_specs=pl.BlockSpec((nblk, 256), lambda b: (b, 0)),
        out_shape=jax.ShapeDtypeStruct((batch * nblk, 256), F32),
        compiler_params=pltpu.CompilerParams(dimension_semantics=("parallel",), vmem_limit_bytes=VMEM_LIMIT),
        name="nsa_compress",
    )(cmp_k, cmp_v, pos2, w1bd, w2bd)


NSA_TQ = 256
NSA_TK = 256
WIN_SPAN = WINDOW + NSA_TQ
LOG2E = 1.4426950408889634


def _nsa_kernel(q_ref, kc_ref, sel_ref, win_ref, oh_ref, g_ref, c2s_ref, eye_ref, o_ref,
                kaug, vsel, vwin, vcmp, qaug, gate_scr, mix_scr):
    tq, tk = NSA_TQ, NSA_TK
    rows3 = NSA_H * tq
    qi = pl.program_id(1)
    n_sel = oh_ref.shape[0] // SEL_BLOCK
    low_half = lax.broadcasted_iota(jnp.int32, (1, LANE), 1) < HEAD_DIM

    def both_halves(v):
        vf = v.astype(F32)
        vr = pltpu.roll(vf, HEAD_DIM, 1)
        return jnp.where(low_half, vf, vr).astype(BF16), jnp.where(low_half, vr, vf).astype(BF16)

    @pl.when(qi == 0)
    def _():
        kaug[:, 0:LANE] = sel_ref[:, 0:LANE]
        kaug[:, LANE:2 * LANE] = oh_ref[...]
        ones = jnp.ones((sel_ref.shape[0], LANE), BF16)
        for dst, src in ((vsel, sel_ref), (vwin, win_ref)):
            for g, dup in enumerate(both_halves(src[:, LANE:2 * LANE])):
                dst[g, :, 0:LANE] = dup
                dst[g, :, LANE:2 * LANE] = ones
        for g, dup in enumerate(both_halves(kc_ref[:, LANE:2 * LANE])):
            vcmp[g] = dup

    q0 = qi * tq
    t3 = q0 + (lax.broadcasted_iota(jnp.int32, (rows3, 1), 0) & (tq - 1))
    t_lane = q0 + lax.broadcasted_iota(jnp.int32, (1, tq), 1)
    gates = jax.nn.sigmoid(g_ref[...])
    for g in range(NSA_G):
        for h in range(NSA_H):
            for br in range(3):
                col = (g * NSA_H + h) * 3 + br
                gate_scr[g, br, h * tq:(h + 1) * tq, :] = jnp.broadcast_to(gates[:, col:col + 1], (tq, LANE))
    kc_b = kc_ref[:, 0:LANE].astype(BF16)
    for g in range(NSA_G):
        glanes = slice(g * HEAD_DIM, (g + 1) * HEAD_DIM)
        qaug[g, :, 0:LANE] = jnp.zeros((rows3, LANE), BF16)
        for h in range(NSA_H):
            c = (NSA_H * g + h) * HEAD_DIM
            qaug[g, h * tq:(h + 1) * tq, glanes] = q_ref[:, c:c + HEAD_DIM]

    ks = pl.multiple_of(jnp.maximum(q0 - WINDOW, 0), tq)
    kw = win_ref[pl.ds(ks, WIN_SPAN), 0:LANE]
    t1 = q0 + lax.broadcasted_iota(jnp.int32, (tq, 1), 0)
    diff = t1 - (ks + lax.broadcasted_iota(jnp.int32, (1, WIN_SPAN), 1))
    wbias = jnp.where((diff >= 0) & (diff < WINDOW), 0.0, NEG)
    wbias = jnp.concatenate([wbias] * NSA_H, axis=0)
    sws = [_nt(qaug[g, :, 0:LANE], kw) + wbias for g in range(NSA_G)]
    scs = [_nt(qaug[g, :, 0:LANE], kc_b) for g in range(NSA_G)]
    pws = [jnp.exp2(sw - jnp.max(sw, axis=1, keepdims=True)).astype(BF16) for sw in sws]
    a_wins = [_dot(pws[g], vwin[g, pl.ds(ks, WIN_SPAN), :]) for g in range(NSA_G)]
    o_win = [a[:, 0:LANE] * (gate_scr[g, 2] / a[:, LANE:2 * LANE]) for g, a in enumerate(a_wins)]

    for g in range(NSA_G):
        s = scs[g]
        ncol = lax.broadcasted_iota(jnp.int32, (1, s.shape[1]), 1)
        cmask = (CMP_STRIDE * ncol + (CMP_BLOCK - 1)) <= t3
        sm = jnp.where(cmask, s, NEG)
        m = jnp.max(sm, axis=1, keepdims=True)
        e = jnp.where(cmask, jnp.exp2(sm - m), 0.0)
        p = e / jnp.maximum(jnp.sum(e, axis=1, keepdims=True), 1e-30)
        o_cmp = _dot(p.astype(BF16), vcmp[g])
        mix_scr[g] = gate_scr[g, 0] * o_cmp + o_win[g]

        psum = p[0:tq] + p[tq:2 * tq] + p[2 * tq:3 * tq]
        c2s = c2s_ref[...]
        imp = sum(_nt(c2s, term) for term in _split3(psum))
        jb = lax.broadcasted_iota(jnp.int32, (n_sel, 1), 0)
        cur = t_lane // SEL_BLOCK
        imp = jnp.where((jb == 0) | (jb == cur) | (jb == cur - 1), FORCE_SCORE, imp)
        imp = jnp.where(jb > cur, -FORCE_SCORE, imp)
        rank = jnp.zeros(imp.shape, F32)
        for i in range(n_sel):
            ri = imp[i:i + 1, :]
            ahead = (ri > imp) | ((ri == imp) & (jb > i))
            rank = rank + jnp.where(ahead, 1.0, 0.0)
        sel_t = jnp.where(rank < float(min(SEL_TOP_N, n_sel)), 1.0, 0.0).astype(BF16)
        sel_t = jnp.concatenate([sel_t, jnp.zeros((LANE - n_sel, tq), BF16)], axis=0)
        sel = _nt(eye_ref[...], sel_t)
        bias = ((sel - 1.0) * (-SEL_NEG)).astype(BF16)
        for h in range(NSA_H):
            qaug[g, h * tq:(h + 1) * tq, LANE:2 * LANE] = bias

    last = (q0 + tq - 1) // tk

    def selected(n_chunks):
        lo = (n_chunks - 1) * tk
        kpos = lo + lax.broadcasted_iota(jnp.int32, (1, tk), 1)
        cbias = jnp.concatenate([jnp.where(kpos <= t1, 0.0, NEG)] * NSA_H, axis=0)
        heads = []
        sbs = [_nt(qaug[g], kaug[lo:lo + tk, :]) + cbias for g in range(NSA_G)]
        sas = [_nt(qaug[g], kaug[0:lo, :]) for g in range(NSA_G)] if lo else None
        for g in range(NSA_G):
            sb = sbs[g]
            blocks = [sb[:, j * LANE:(j + 1) * LANE] for j in range(tk // LANE)]
            if lo:
                sa = sas[g]
                blocks += [sa[:, j * LANE:(j + 1) * LANE] for j in range(lo // LANE)]
            m = jnp.max(functools.reduce(jnp.maximum, blocks), axis=1, keepdims=True)
            acc = _dot(jnp.exp2(sb - m).astype(BF16), vsel[g, lo:lo + tk, :])
            if lo:
                acc = acc + _dot(jnp.exp2(sa - m).astype(BF16), vsel[g, 0:lo, :])
            mix = mix_scr[g] + acc[:, 0:LANE] * (gate_scr[g, 1] / acc[:, LANE:2 * LANE])
            heads += [mix[h * tq:(h + 1) * tq] for h in range(NSA_H)]
        low = lax.broadcasted_iota(jnp.int32, (1, LANE), 1) < HEAD_DIM
        for j in range(len(heads) // 2):
            pair = jnp.where(low, heads[2 * j], heads[2 * j + 1])
            o_ref[:, j * LANE:(j + 1) * LANE] = pair.astype(o_ref.dtype)

    for n_chunks in range(1, kaug.shape[0] // tk + 1):
        pl.when(last == n_chunks - 1)(functools.partial(selected, n_chunks))


def _nsa(q, kc, sel, win, small, onehot, c2s_t, eye, batch, seq):
    nq = seq // NSA_TQ
    nblk = seq // CMP_STRIDE
    rows3 = NSA_H * NSA_TQ
    qrow = lambda b, i: (b * nq + i, 0)
    per_b = lambda b, i: (b, 0)
    const = lambda b, i: (0, 0)
    return pl.pallas_call(
        _nsa_kernel,
        grid=(batch, nq),
        in_specs=[pl.BlockSpec((NSA_TQ, NSA_WIDTH), qrow),
                  pl.BlockSpec((nblk, 256), per_b),
                  pl.BlockSpec((seq, 256), per_b),
                  pl.BlockSpec((seq, 256), per_b),
                  pl.BlockSpec((seq, LANE), const),
                  pl.BlockSpec((NSA_TQ, LANE), qrow),
                  pl.BlockSpec(c2s_t.shape, const),
                  pl.BlockSpec((NSA_TQ, NSA_TQ), const)],
        out_specs=pl.BlockSpec((NSA_TQ, NSA_WIDTH), qrow),
        out_shape=jax.ShapeDtypeStruct((batch * seq, NSA_WIDTH), BF16),
        scratch_shapes=[pltpu.VMEM((seq, 2 * LANE), BF16),
                        pltpu.VMEM((NSA_G, seq, 2 * LANE), BF16),
                        pltpu.VMEM((NSA_G, seq, 2 * LANE), BF16),
                        pltpu.VMEM((NSA_G, nblk, LANE), BF16),
                        pltpu.VMEM((NSA_G, rows3, 2 * LANE), BF16),
                        pltpu.VMEM((NSA_G, 3, rows3, LANE), F32),
                        pltpu.VMEM((NSA_G, rows3, LANE), F32)],
        compiler_params=pltpu.CompilerParams(dimension_semantics=("parallel", "arbitrary"),
                                             vmem_limit_bytes=VMEM_LIMIT),
        name="nsa_attention",
    )(q, kc, sel, win, onehot, small, c2s_t, eye)


GLA_TILE = 256
GLA_PRE_TILES = 2
GLA_SCAN_CHUNKS = 8


def _log_sigmoid(x):
    return jnp.minimum(x, 0.0) - jnp.log(1.0 + jnp.exp(-jnp.abs(x)))


def _gla_kernel(q_ref, k_ref, v_ref, og_ref, small_ref, wg_ref, bg_ref, ng_ref, ltri_ref, lsum_ref, bd_ref,
                o_ref, qd_s, ke_s, dec_s, oacc_s, st_s):
    seq = q_ref.shape[0]
    ch = GLA_CHUNK
    nh = GLA_HEADS
    tile = GLA_TILE
    qk_head = lax.broadcasted_iota(jnp.int32, (1, GLA_QK_PAD), 1) // GLA_DK
    low = lax.broadcasted_iota(jnp.int32, (1, LANE), 1) < GLA_DV
    ri = lax.broadcasted_iota(jnp.int32, (tile, 1), 0)
    cj = lax.broadcasted_iota(jnp.int32, (1, tile), 1)
    intra = (cj <= ri) & ((cj // ch) == (ri // ch))
    intra = jnp.concatenate([intra] * nh, axis=1)

    def pre(i, _):
        staged = [decays(GLA_PRE_TILES * i + j) for j in range(GLA_PRE_TILES)]
        for args in staged:
            intra_chunk(*args)
        return 0

    def decays(i):
        rs = pl.ds(pl.multiple_of(i * tile, tile), tile)
        x_hi, x_mid, _ = _split3(small_ref[rs, :])
        w_hi, w_mid, _ = _split3(wg_ref[...])
        z = _dot(x_hi, w_hi) + (_dot(x_hi, w_mid) + _dot(x_mid, w_hi)) + bg_ref[...]
        gl = _log_sigmoid(z) * (1.0 / GLA_GATE_NORM)
        g_hi, g_mid, _ = _split3(gl)
        bc = _dot(ltri_ref[...], g_hi) + _dot(ltri_ref[...], g_mid)
        bl = _dot(lsum_ref[...], g_hi) + _dot(lsum_ref[...], g_mid)
        dec = jnp.exp(bl)
        qd = (q_ref[rs, :].astype(F32) * (GLA_DK ** -0.5) * jnp.exp(bc)).astype(BF16)
        k_inv = k_ref[rs, :].astype(F32) * jnp.exp(-bc)
        ki = k_inv.astype(BF16)
        qd_s[rs, :] = qd
        ke_s[rs, :] = (k_inv * dec).astype(BF16)
        dec_s[rs, :] = dec
        return rs, qd, ki

    def intra_chunk(rs, qd, ki):
        k_heads = jnp.concatenate([jnp.where(qk_head == h, ki, jnp.zeros_like(ki)) for h in range(nh)], axis=0)
        a = jnp.where(intra, _nt(qd, k_heads), 0.0).astype(BF16)
        for pr in range(nh // 2):
            lanes = slice(pr * LANE, (pr + 1) * LANE)
            vp = v_ref[rs, lanes]
            zero = jnp.zeros_like(vp)
            v_pair = jnp.concatenate([jnp.where(low, vp, zero), jnp.where(low, zero, vp)], axis=0)
            oacc_s[rs, lanes] = _dot(a[:, 2 * pr * tile:(2 * pr + 2) * tile], v_pair)

    lax.fori_loop(0, seq // (GLA_PRE_TILES * tile), pre, 0)

    st_mask = (lax.broadcasted_iota(jnp.int32, (GLA_WIDTH, 1), 0) // GLA_DV) == qk_head
    st_s[...] = jnp.zeros(st_s.shape, F32)
    per_iter = GLA_SCAN_CHUNKS
    span = per_iter * ch

    def scan(i, _):
        st = st_s[...]
        base = pl.multiple_of(i * span, span)
        dts = [_tn(v_ref[pl.ds(base + c * ch, ch), :], ke_s[pl.ds(base + c * ch, ch), :]) for c in range(per_iter)]
        for c in range(per_iter):
            rs = pl.ds(base + c * ch, ch)
            oacc_s[rs, :] = oacc_s[rs, :] + _nt(qd_s[rs, :], st.astype(BF16))
            st = st * dec_s[pl.ds(base + c * ch, 1), :] + jnp.where(st_mask, dts[c], 0.0)
        st_s[...] = st
        return 0

    lax.fori_loop(0, seq // span, scan, 0)

    def post(i, _):
        tiles = [pl.ds(pl.multiple_of((2 * i + half) * tile, tile), tile) for half in range(2)]
        os_ = [oacc_s[rs, :] for rs in tiles]
        mss = [_dot((o * o).astype(BF16), bd_ref[...]) for o in os_]
        for rs, o, ms in zip(tiles, os_, mss):
            og = og_ref[rs, :].astype(F32)
            y = o * lax.rsqrt(ms + RMS_EPS) * ng_ref[...] * (og * jax.nn.sigmoid(og))
            o_ref[rs, :] = y.astype(o_ref.dtype)
        return 0

    lax.fori_loop(0, seq // (2 * tile), post, 0)


def _gla(gq, gk, gv, og, small, wg, bg, ng, ltri, lsum, bd, batch, seq):
    per_b = lambda b: (b, 0)
    const = lambda b: (0, 0)
    return pl.pallas_call(
        _gla_kernel,
        grid=(batch,),
        in_specs=[pl.BlockSpec((seq, GLA_QK_PAD), per_b), pl.BlockSpec((seq, GLA_QK_PAD), per_b),
                  pl.BlockSpec((seq, GLA_WIDTH), per_b), pl.BlockSpec((seq, GLA_WIDTH), per_b),
                  pl.BlockSpec((seq, LANE), per_b),
                  pl.BlockSpec(wg.shape, const), pl.BlockSpec(bg.shape, const), pl.BlockSpec(ng.shape, const),
                  pl.BlockSpec(ltri.shape, const), pl.BlockSpec(lsum.shape, const), pl.BlockSpec(bd.shape, const)],
        out_specs=pl.BlockSpec((seq, GLA_WIDTH), per_b),
        out_shape=jax.ShapeDtypeStruct((batch * seq, GLA_WIDTH), BF16),
        scratch_shapes=[pltpu.VMEM((seq, GLA_QK_PAD), BF16), pltpu.VMEM((seq, GLA_QK_PAD), BF16),
                        pltpu.VMEM((seq, GLA_QK_PAD), F32),
                        pltpu.VMEM((seq, GLA_WIDTH), F32), pltpu.VMEM((GLA_WIDTH, GLA_QK_PAD), F32)],
        compiler_params=pltpu.CompilerParams(dimension_semantics=("parallel",), vmem_limit_bytes=VMEM_LIMIT),
        name="gla",
    )(gq, gk, gv, og, small, wg, bg, ng, ltri, lsum, bd)


POOL_TILE = 256
POOL_HALO = 16


def _pool_kernel(u_ref, w_ref, scale_ref, o_ref):
    seq = u_ref.shape[0]
    lane = lax.broadcasted_iota(jnp.int32, (1, POOL_WIDTH), 1)
    grp = lane // POOL_GROUP
    win = jnp.where(grp == 0, POOL_WINDOWS[0],
                    jnp.where(grp == 1, POOL_WINDOWS[1], jnp.where(grp == 2, POOL_WINDOWS[2], POOL_WINDOWS[3])))
    for i in range(seq // POOL_TILE):
        r0 = i * POOL_TILE
        u = u_ref[r0:r0 + POOL_TILE, :].astype(F32)
        if i > 0:
            x = u_ref[r0 - POOL_HALO:r0 + POOL_TILE, :].astype(F32)
        else:
            x = jnp.concatenate([jnp.zeros((POOL_HALO, POOL_WIDTH), F32), u], axis=0)
        tot = x
        span = 1
        for gi, w in enumerate(POOL_WINDOWS):
            while span < w:
                x = x + pltpu.roll(x, span, 0)
                span *= 2
            tot = jnp.where(grp == gi, x, tot)
        tot = tot[POOL_HALO:, :]
        t = r0 + lax.broadcasted_iota(jnp.int32, (POOL_TILE, 1), 0)
        cnt = jnp.minimum(t + 1, win).astype(F32)
        pooled = tot / cnt - u
        o_ref[r0:r0 + POOL_TILE, :] = (_dot(pooled.astype(BF16), w_ref[...]) * scale_ref[...]).astype(o_ref.dtype)


def _pool(u, wbd, scale, batch, seq):
    per_b = lambda b: (b, 0)
    const = lambda b: (0, 0)
    return pl.pallas_call(
        _pool_kernel,
        grid=(batch,),
        in_specs=[pl.BlockSpec((seq, POOL_WIDTH), per_b),
                  pl.BlockSpec(wbd.shape, const), pl.BlockSpec(scale.shape, const)],
        out_specs=pl.BlockSpec((seq, POOL_WIDTH), per_b),
        out_shape=jax.ShapeDtypeStruct((batch * seq, POOL_WIDTH), BF16),
        compiler_params=pltpu.CompilerParams(dimension_semantics=("parallel",), vmem_limit_bytes=VMEM_LIMIT),
        name="pool",
    )(u, wbd, scale)


FFN_CHUNK = 1024


def _mix_ffn_kernel(nsa_ref, gla_ref, pool_ref, h_ref, wo_ref, g1_ref, b1_ref, wu_ref, wd_ref, g2_ref, b2_ref,
                    o_ref):
    half = h_ref.shape[0] // 2
    parts = [slice(0, half), slice(half, 2 * half)]
    mixed = []
    for rs in parts:
        acc = _dot(nsa_ref[rs, :], wo_ref[0:NSA_WIDTH, :])
        acc = acc + _dot(gla_ref[rs, :], wo_ref[NSA_WIDTH:NSA_WIDTH + GLA_WIDTH, :])
        mixed.append(acc + _dot(pool_ref[rs, :], wo_ref[NSA_WIDTH + GLA_WIDTH:, :]))
    hs = [_layer_norm(ALPHA * h_ref[rs, :] + a, g1_ref[...], b1_ref[...]) for rs, a in zip(parts, mixed)]
    hbs = [h.astype(BF16) for h in hs]
    accs = [ALPHA * h for h in hs]
    for c in range(D_FF // FFN_CHUNK):
        cs = slice(c * FFN_CHUNK, (c + 1) * FFN_CHUNK)
        us = [jnp.maximum(_dot(hb, wu_ref[:, cs]), 0.0) for hb in hbs]
        accs = [acc + _dot((u * u).astype(BF16), wd_ref[cs, :]) for acc, u in zip(accs, us)]
    for rs, acc in zip(parts, accs):
        o_ref[rs, :] = _layer_norm(acc, g2_ref[...], b2_ref[...])


def _mix_ffn(nsa, gla, pool, h, wo, g1, b1, wu, wd, g2, b2, layer, tm):
    m = h.shape[0]
    row = lambda i: (i, 0)
    const = lambda i: (0, 0)
    vec = pl.BlockSpec((1, D_MODEL), const)
    resident = lambda a: pl.BlockSpec((None,) + a.shape[1:], lambda i: (layer, 0, 0), pipeline_mode=pl.Buffered(1))
    return pl.pallas_call(
        _mix_ffn_kernel,
        grid=(m // tm,),
        in_specs=[pl.BlockSpec((tm, NSA_WIDTH), row), pl.BlockSpec((tm, GLA_WIDTH), row),
                  pl.BlockSpec((tm, POOL_WIDTH), row), pl.BlockSpec((tm, D_MODEL), row),
                  resident(wo), vec, vec, resident(wu), resident(wd), vec, vec],
        out_specs=pl.BlockSpec((tm, D_MODEL), row),
        out_shape=jax.ShapeDtypeStruct((m, D_MODEL), F32),
        compiler_params=pltpu.CompilerParams(dimension_semantics=("parallel",), vmem_limit_bytes=FFN_VMEM_LIMIT),
        name="out_proj_ffn",
    )(nsa, gla, pool, h, wo, g1, b1, wu, wd, g2, b2)


def _rope_tables(seq, scale):
    half = ROPE_DIM // 2
    inv = ROPE_THETA ** (-jnp.arange(half, dtype=F32) * 2.0 / ROPE_DIM)
    ang = jnp.arange(seq).astype(F32)[:, None] * inv[None, :]
    cos, sin = jnp.cos(ang), jnp.sin(ang)
    ones = jnp.ones((seq, HEAD_DIM - ROPE_DIM), F32)
    zeros = jnp.zeros((seq, HEAD_DIM - ROPE_DIM), F32)
    zh = jnp.zeros((seq, half), F32)
    a = jnp.concatenate([cos, cos, ones], axis=1)
    bm = jnp.concatenate([-sin, zh, zeros], axis=1)
    cm = jnp.concatenate([zh, sin, zeros], axis=1)
    reps = LANE // HEAD_DIM
    return tuple(jnp.tile(t, (1, reps)) * scale for t in (a, bm, cm))


def _static_tables(seq):
    n_sel = seq // SEL_BLOCK
    nblk = seq // CMP_STRIDE
    n_cmp = (seq - CMP_BLOCK) // CMP_STRIDE + 1
    onehot = np.zeros((seq, LANE), np.float32)
    onehot[np.arange(seq), np.arange(seq) // SEL_BLOCK] = 1.0
    c_lo = CMP_STRIDE * np.arange(n_cmp)[:, None]
    s_lo = SEL_BLOCK * np.arange(n_sel)[None, :]
    overlap = np.clip(np.minimum(c_lo + CMP_BLOCK, s_lo + SEL_BLOCK) - np.maximum(c_lo, s_lo), 0, None)
    c2s_t = np.zeros((n_sel, nblk), np.float32)
    c2s_t[:, :n_cmp] = (overlap.astype(np.float32) / CMP_BLOCK).T
    eye = np.eye(NSA_TQ, dtype=np.float32)
    r = np.arange(GLA_TILE)
    same = (r[:, None] // GLA_CHUNK) == (r[None, :] // GLA_CHUNK)
    ltri = (same & (r[None, :] <= r[:, None])).astype(np.float32)
    lsum = same.astype(np.float32)
    hv = np.arange(GLA_WIDTH) // GLA_DV
    bd = (hv[:, None] == hv[None, :]).astype(np.float32) / GLA_DV
    as_bf = lambda a: jnp.asarray(a, BF16)
    return dict(onehot=as_bf(onehot), c2s_t=as_bf(c2s_t), eye=as_bf(eye), ltri=as_bf(ltri),
                lsum=as_bf(lsum), bd=as_bf(bd))


def _block_diag2(w):
    z = jnp.zeros_like(w)
    return jnp.concatenate([jnp.concatenate([w, z], axis=-1), jnp.concatenate([z, w], axis=-1)], axis=-2)


def _layer_weights(w_in, cmp_pos, cmp_w1, cmp_w2, w_gate2, b_gate, norm_g, pool_w, pool_scale):
    o_nq, o_nkv, o_gate = 0, NSA_WIDTH, NSA_WIDTH + 768
    o_gq = o_gate + 18
    o_gk = o_gq + 192
    o_gv = o_gk + 192
    o_glr = o_gv + GLA_WIDTH
    o_og = o_glr + GLA_RANK
    o_pu = o_og + GLA_WIDTH
    p_sm, p_gq, p_gk, p_gv, p_og, p_pu = 9 * LANE, 10 * LANE, 12 * LANE, 14 * LANE, 17 * LANE, 20 * LANE
    segments = ((0, o_nq, o_gate - o_nq), (p_sm + SMALL_GATE0, o_gate, 18), (p_sm + SMALL_GLR0, o_glr, GLA_RANK),
                (p_gq, o_gq, 192), (p_gk, o_gk, 192), (p_gv, o_gv, GLA_WIDTH), (p_og, o_og, GLA_WIDTH),
                (p_pu, o_pu, POOL_WIDTH))
    wp = _permute_rows(w_in.T, segments)
    pos2 = jnp.concatenate([cmp_pos, cmp_pos], axis=-1)
    w1bd = _block_diag2(cmp_w1.reshape(2, CMP_BLOCK, HEAD_DIM, cmp_w1.shape[-1])).astype(BF16)
    w2bd = _block_diag2(cmp_w2).astype(BF16)
    wg = jnp.zeros((LANE, GLA_QK_PAD), F32).at[SMALL_GLR0:SMALL_GLR0 + GLA_RANK, :192].set(w_gate2)
    bg = jnp.zeros((1, GLA_QK_PAD), F32).at[0, :192].set(b_gate)
    ng = norm_g.reshape(1, GLA_WIDTH)
    pw = jnp.zeros((POOL_WIDTH, POOL_WIDTH), F32)
    for gi in range(len(POOL_WINDOWS)):
        sl = slice(gi * POOL_GROUP, (gi + 1) * POOL_GROUP)
        pw = pw.at[sl, sl].set(pool_w[gi])
    return dict(wp=wp, pos2=pos2, w1bd=w1bd, w2bd=w2bd, wg=wg, bg=bg, ng=ng, pw=pw.astype(BF16),
                ps=pool_scale.reshape(1, POOL_WIDTH))


@jax.jit
def kernel(x, w_in, cmp_pos, cmp_w1, cmp_w2, gla_w_gate2, gla_b_gate, gla_norm_g, pool_w, pool_scale,
           w_out, ln1_g, ln1_b, w_up, w_down, ln2_g, ln2_b):
    batch, seq, d = x.shape
    assert d == D_MODEL and seq % (GLA_PRE_TILES * GLA_TILE) == 0 and seq >= WIN_SPAN and seq // SEL_BLOCK <= LANE
    tm = 512
    st = _static_tables(seq)
    rope_q = _rope_tables(seq, HEAD_DIM ** -0.5 * LOG2E)
    rope_k = _rope_tables(seq, 1.0)
    wo_all, wu_all, wd_all = w_out.astype(BF16), w_up.astype(BF16), w_down.astype(BF16)
    h = x.reshape(batch * seq, d)
    for l in range(w_in.shape[0]):
        lw = _layer_weights(w_in[l], cmp_pos[l], cmp_w1[l], cmp_w2[l], gla_w_gate2[l], gla_b_gate[l],
                            gla_norm_g[l], pool_w[l], pool_scale[l])
        q, cmp_k, cmp_v, sel, win, small, gq, gk, gv, og, pu = _proj(h, lw["wp"], rope_q, rope_k, seq, 2 * tm)
        kc = _compress(cmp_k, cmp_v, lw["pos2"], lw["w1bd"], lw["w2bd"], batch, seq)
        nsa = _nsa(q, kc, sel, win, small, st["onehot"], st["c2s_t"], st["eye"], batch, seq)
        gla = _gla(gq, gk, gv, og, small, lw["wg"], lw["bg"], lw["ng"], st["ltri"], st["lsum"], st["bd"],
                   batch, seq)
        pool = _pool(pu, lw["pw"], lw["ps"], batch, seq)
        h = _mix_ffn(nsa, gla, pool, h, wo_all, ln1_g[l].reshape(1, d), ln1_b[l].reshape(1, d),
                     wu_all, wd_all, ln2_g[l].reshape(1, d), ln2_b[l].reshape(1, d), l, 2 * tm)
    return h.reshape(batch, seq, d)
```
